```python
import math
import jax, jax.numpy as jnp
from jax import lax
import numpy as np

D_MODEL = 1024
BATCH = 8
SEQ = 4096
DEPTH = 2

HEAD_DIM = 64
N_HEADS_A = D_MODEL // HEAD_DIM
MOBA_BLOCK = 256
MOBA_TOPK = 3
MOBA_Q_CHUNK = 16
N_HEADS_B = D_MODEL // HEAD_DIM
N_KV_B = max(1, N_HEADS_B // 8)
WINDOW = 128
D_FF = 256 * math.ceil(8 * D_MODEL / 3 / 256)
N_EXPERTS = 8
MOE_TOPK = 2
D_FF_EXPERT = 7 * D_MODEL // 2
N_A = max(1, DEPTH // 2)
N_B = DEPTH - N_A
N_DENSE = (DEPTH + 1) // 2
N_MOE = DEPTH // 2
DEEPNORM_ALPHA = (2 * DEPTH) ** 0.25
DEEPNORM_BETA = (8 * DEPTH) ** -0.25
LN_EPS = 1e-5
NEG_INF = -1e30

kernel_name = 'hybrid_moba_swa_sinks_yoco_deepnorm_moe'


def layer_norm(x, g, b):
    xf = x.astype(jnp.float32)
    mu = xf.mean(-1, keepdims=True)
    var = jnp.square(xf - mu).mean(-1, keepdims=True)
    return ((xf - mu) * lax.rsqrt(var + LN_EPS) * g.astype(jnp.float32) + b.astype(jnp.float32)).astype(x.dtype)


def alibi_slopes(n):
    return jnp.exp2(-8.0 * jnp.arange(1, n + 1, dtype=jnp.float32) / n)


def moba_attention(x, w_qkv, w_o):
    B, T, _ = x.shape
    H, hd, BS, QC = N_HEADS_A, HEAD_DIM, MOBA_BLOCK, MOBA_Q_CHUNK
    q, k, v = jnp.split(x @ w_qkv, 3, axis=-1)
    Tp = -(-T // BS) * BS
    pad = Tp - T

    def heads(t):
        t = jnp.pad(t, ((0, 0), (0, pad), (0, 0)))
        return t.reshape(B, Tp, H, hd).transpose(0, 2, 1, 3)

    q, k, v = heads(q), heads(k), heads(v)
    NB = Tp // BS
    k_blocks = k.reshape(B, H, NB, BS, hd)
    v_blocks = v.reshape(B, H, NB, BS, hd)
    k_mean = k_blocks.astype(jnp.float32).mean(axis=3)
    gate = jnp.einsum('bhtd,bhnd->bhtn', q.astype(jnp.float32), k_mean)
    q_blk = jnp.arange(Tp) // BS
    past = jnp.arange(NB)[None, :] < q_blk[:, None]
    gate = jnp.where(past, gate, NEG_INF)
    k_sel = min(MOBA_TOPK, NB)
    _, sel = lax.top_k(gate, k_sel)
    NC = Tp // QC
    q_c = (q * hd ** -0.5).reshape(B, H, NC, QC, hd).transpose(2, 0, 1, 3, 4)
    sel_c = sel.reshape(B, H, NC, QC, k_sel).transpose(2, 0, 1, 3, 4)
    slopes = alibi_slopes(H)
    pos_in_blk = jnp.arange(BS)
    gather = jax.vmap(jax.vmap(lambda blocks, ix: blocks[ix]))

    def chunk(args):
        c, qc, ix = args
        t = c * QC + jnp.arange(QC)
        blk = (c * QC) // BS
        k_own = lax.dynamic_index_in_dim(k_blocks, blk, axis=2, keepdims=False)
        v_own = lax.dynamic_index_in_dim(v_blocks, blk, axis=2, keepdims=False)
        d_own = (t[:, None] - (blk * BS + pos_in_blk)[None, :]).astype(jnp.float32)
        l_own = jnp.einsum('bhqd,bhsd->bhqs', qc, k_own).astype(jnp.float32) - slopes[None, :, None, None] * d_own
        l_own = jnp.where(d_own >= 0, l_own, NEG_INF)
        kg = gather(k_blocks, ix)
        vg = gather(v_blocks, ix)
        d_sel = (t[None, None, :, None, None] - (ix[..., None] * BS + pos_in_blk)).astype(jnp.float32)
        l_sel = jnp.einsum('bhqd,bhqnsd->bhqns', qc, kg).astype(jnp.float32) - slopes[None, :, None, None, None] * d_sel
        l_sel = jnp.where((ix < blk)[..., None], l_sel, NEG_INF).reshape(B, H, QC, k_sel * BS)
        p = jax.nn.softmax(jnp.concatenate([l_own, l_sel], axis=-1), axis=-1)
        p_own = p[..., :BS].astype(v.dtype)
        p_sel = p[..., BS:].reshape(B, H, QC, k_sel, BS).astype(v.dtype)
        return jnp.einsum('bhqs,bhsd->bhqd', p_own, v_own) + jnp.einsum('bhqns,bhqnsd->bhqd', p_sel, vg)

    o = lax.map(chunk, (jnp.arange(NC), q_c, sel_c))
    o = o.transpose(1, 0, 3, 2, 4).reshape(B, Tp, H * hd)[:, :T]
    return o @ w_o


def swa_sinks_attention(x, w_q, w_o, sinks, k, v):
    B, T, _ = x.shape
    H, KV, hd, W = N_HEADS_B, N_KV_B, HEAD_DIM, WINDOW
    G = H // KV
    NB = T // W
    q = (x @ w_q).reshape(B, NB, W, KV, G, hd) * hd ** -0.5

    def band(t):
        tb = t.reshape(B, NB, W, KV, hd)
        prev = jnp.pad(tb, ((0, 0), (1, 0), (0, 0), (0, 0), (0, 0)))[:, :-1]
        return jnp.concatenate([prev, tb], axis=2)

    kb, vb = band(k), band(v)
    logits = jnp.einsum('bnqkgd,bnskd->bkgnqs', q, kb).astype(jnp.float32)
    s_idx = jnp.arange(2 * W)
    d = W + jnp.arange(W)[:, None] - s_idx[None, :]
    slopes = alibi_slopes(H).reshape(KV, G)[None, :, :, None, None, None]
    logits = logits - slopes * d.astype(jnp.float32)
    key_pos = jnp.arange(NB)[:, None, None] * W - W + s_idx[None, None, :]
    mask = ((d >= 0) & (d < W))[None] & (key_pos >= 0)
    logits = jnp.where(mask, logits, NEG_INF)
    sink = jnp.broadcast_to(sinks.reshape(KV, G).astype(jnp.float32)[None, :, :, None, None, None],
                            logits.shape[:-1] + (1,))
    p = jax.nn.softmax(jnp.concatenate([logits, sink], axis=-1), axis=-1)[..., :-1]
    o = jnp.einsum('bkgnqs,bnskd->bnqkgd', p.astype(v.dtype), vb).reshape(B, T, H * hd)
    return o @ w_o


def swiglu(x, w_gate, w_up, w_down):
    return (jax.nn.silu(x @ w_gate) * (x @ w_up)) @ w_down


def moe_swiglu(x, w_router, w_gate, w_up, w_down):
    B, T, D = x.shape
    xt = x.reshape(B * T, D)
    logits = (xt @ w_router).astype(jnp.float32)
    top_val, top_idx = lax.top_k(logits, MOE_TOPK)
    top_w = jax.nn.softmax(top_val, axis=-1)
    gates = jnp.sum(jax.nn.one_hot(top_idx, N_EXPERTS, dtype=jnp.float32) * top_w[..., None], axis=1)
    y = jnp.zeros_like(xt)
    for e in range(N_EXPERTS):
        y = y + gates[:, e:e + 1].astype(x.dtype) * swiglu(xt, w_gate[e], w_up[e], w_down[e])
    return y.reshape(B, T, D)


def setup_inputs(seed: int = 0) -> dict:
    key = jax.random.key(seed)
    ks = jax.random.split(key, 20)
    D = D_MODEL
    HA = N_HEADS_A * HEAD_DIM
    HB = N_HEADS_B * HEAD_DIM
    KVB = N_KV_B * HEAD_DIM
    beta = DEEPNORM_BETA

    def nrm(k, shape, fan_in, scale=1.0):
        return jax.random.normal(k, shape, jnp.float32) * (scale * fan_in ** -0.5)

    x = jax.random.normal(ks[0], (BATCH, SEQ, D), jnp.float32)
    w_qkv_a = jnp.concatenate([nrm(ks[1], (N_A, D, 2 * HA), D), nrm(ks[2], (N_A, D, HA), D, beta)], axis=-1)
    w_o_a = nrm(ks[3], (N_A, HA, D), HA, beta)
    w_kv_shared = jnp.concatenate([nrm(ks[4], (D, KVB), D), nrm(ks[5], (D, KVB), D, beta)], axis=-1)
    w_q_b = nrm(ks[6], (N_B, D, HB), D)
    w_o_b = nrm(ks[7], (N_B, HB, D), HB, beta)
    sinks_b = 0.5 * jax.random.normal(ks[8], (N_B, N_HEADS_B), jnp.float32)
    w_gate_d = nrm(ks[9], (N_DENSE, D, D_FF), D, beta)
    w_up_d = nrm(ks[10], (N_DENSE, D, D_FF), D, beta)
    w_down_d = nrm(ks[11], (N_DENSE, D_FF, D), D_FF, beta)
    w_router = nrm(ks[12], (N_MOE, D, N_EXPERTS), D)
    w_gate_e = nrm(ks[13], (N_MOE, N_EXPERTS, D, D_FF_EXPERT), D, beta)
    w_up_e = nrm(ks[14], (N_MOE, N_EXPERTS, D, D_FF_EXPERT), D, beta)
    w_down_e = nrm(ks[15], (N_MOE, N_EXPERTS, D_FF_EXPERT, D), D_FF_EXPERT, beta)
    ln_gain = 1.0 + 0.05 * jax.random.normal(ks[16], (DEPTH, 2, D), jnp.float32)
    ln_bias = 0.02 * jax.random.normal(ks[17], (DEPTH, 2, D), jnp.float32)
    return {'x': x, 'w_qkv_a': w_qkv_a, 'w_o_a': w_o_a, 'w_kv_shared': w_kv_shared,
            'w_q_b': w_q_b, 'w_o_b': w_o_b, 'sinks_b': sinks_b,
            'w_gate_d': w_gate_d, 'w_up_d': w_up_d, 'w_down_d': w_down_d,
            'w_router': w_router, 'w_gate_e': w_gate_e, 'w_up_e': w_up_e, 'w_down_e': w_down_e,
            'ln_gain': ln_gain, 'ln_bias': ln_bias}


def reference(x, w_qkv_a, w_o_a, w_kv_shared, w_q_b, w_o_b, sinks_b,
              w_gate_d, w_up_d, w_down_d, w_router, w_gate_e, w_up_e, w_down_e,
              ln_gain, ln_bias):
    B, T, _ = x.shape
    alpha = DEEPNORM_ALPHA
    k_sh = v_sh = None
    for i in range(DEPTH):
        if i < N_A:
            mix = moba_attention(x, w_qkv_a[i], w_o_a[i])
        else:
            if i == N_A:
                kv = (x @ w_kv_shared).reshape(B, T, 2, N_KV_B, HEAD_DIM)
                k_sh, v_sh = kv[:, :, 0], kv[:, :, 1]
            j = i - N_A
            mix = swa_sinks_attention(x, w_q_b[j], w_o_b[j], sinks_b[j], k_sh, v_sh)
        x = layer_norm(alpha * x + mix, ln_gain[i, 0], ln_bias[i, 0])
        if i % 2 == 0:
            ffn = swiglu(x, w_gate_d[i // 2], w_up_d[i // 2], w_down_d[i // 2])
        else:
            ffn = moe_swiglu(x, w_router[i // 2], w_gate_e[i // 2], w_up_e[i // 2], w_down_e[i // 2])
        x = layer_norm(alpha * x + ffn, ln_gain[i, 1], ln_bias[i, 1])
    return x
```

```python
import functools

import jax
import jax.numpy as jnp
from jax import lax
from jax.experimental import pallas as pl
from jax.experimental.pallas import tpu as pltpu

HEAD_DIM = 64
MOBA_BLOCK = 256
MOBA_TOPK = 3
N_KV_B = 2
WINDOW = 128
N_EXPERTS = 8
MOE_TOPK = 2
DEPTH = 2
DEEPNORM_ALPHA = (2 * DEPTH) ** 0.25
LN_EPS = 1e-5
NEG_INF = -1e30
PICKED = -3e38
FF_CHUNK = 256
VMEM_LIMIT = 56 * 1024 * 1024

F32 = jnp.float32
BF16 = jnp.bfloat16


def _cparams(*sem):
    return pltpu.CompilerParams(dimension_semantics=sem, vmem_limit_bytes=VMEM_LIMIT)


def _mm_kernel(a_ref, w_ref, o_ref):
    a = a_ref[...].astype(BF16)
    o_ref[...] = jnp.dot(a, w_ref[...], preferred_element_type=F32).astype(o_ref.dtype)


def matmul(a, w, out_dtype, tm=1024, tn=1024):
    m, k = a.shape
    n = w.shape[1]
    tm, tn = min(tm, m), min(tn, n)
    return pl.pallas_call(
        _mm_kernel,
        out_shape=jax.ShapeDtypeStruct((m, n), out_dtype),
        grid=(m // tm, n // tn),
        in_specs=[pl.BlockSpec((tm, k), lambda i, j: (i, 0)),
                  pl.BlockSpec((k, tn), lambda i, j: (0, j))],
        out_specs=pl.BlockSpec((tm, tn), lambda i, j: (i, j)),
        compiler_params=_cparams("parallel", "arbitrary"),
        name="proj",
    )(a, w)


def _layer_norm(y, g, b):
    mu = jnp.mean(y, axis=-1, keepdims=True)
    yc = y - mu
    var = jnp.mean(yc * yc, axis=-1, keepdims=True)
    return yc * lax.rsqrt(var + LN_EPS) * g + b


def _proj_ln_kernel(a_ref, w_ref, x_ref, g_ref, b_ref, o_ref):
    mix = jnp.dot(a_ref[...], w_ref[...], preferred_element_type=F32)
    y = DEEPNORM_ALPHA * x_ref[...] + mix
    o_ref[...] = _layer_norm(y, g_ref[...], b_ref[...])


def proj_ln(a, w, x, g, b, tm=512):
    m, k = a.shape
    d = w.shape[1]
    tm = min(tm, m)
    return pl.pallas_call(
        _proj_ln_kernel,
        out_shape=jax.ShapeDtypeStruct((m, d), F32),
        grid=(m // tm,),
        in_specs=[pl.BlockSpec((tm, k), lambda i: (i, 0)),
                  pl.BlockSpec((k, d), lambda i: (0, 0)),
                  pl.BlockSpec((tm, d), lambda i: (i, 0)),
                  pl.BlockSpec((1, d), lambda i: (0, 0)),
                  pl.BlockSpec((1, d), lambda i: (0, 0))],
        out_specs=pl.BlockSpec((tm, d), lambda i: (i, 0)),
        compiler_params=_cparams("parallel"),
        name="proj_ln",
    )(a, w, x, g.reshape(1, d), b.reshape(1, d))


def _moba_kernel(slope_ref, qT_ref, k_ref, vT_ref, o_ref, sel_ref, bias_ref, dbias_ref):
    nb, bs = k_ref.shape[2], k_ref.shape[3]
    slope = slope_ref[pl.program_id(1)]

    key_i = lax.broadcasted_iota(jnp.int32, (bs, bs), 0)
    qry_i = lax.broadcasted_iota(jnp.int32, (bs, bs), 1)
    rel = (qry_i - key_i).astype(F32)
    bias = slope * rel
    bias_ref[...] = bias
    dbias_ref[...] = jnp.where(rel >= 0, bias, -NEG_INF)

    kmean = jnp.sum(k_ref[0, 0].astype(F32), axis=1) * (1.0 / bs)
    km_hi = kmean.astype(BF16)
    km_lo = (kmean - km_hi.astype(F32)).astype(BF16)
    km2 = jnp.concatenate([km_hi, km_lo], axis=0)
    blk_i = lax.broadcasted_iota(jnp.int32, (nb, bs), 0)

    def q_block(i, carry):
        qT = qT_ref[0, 0, i]
        g2 = jnp.dot(km2, qT, preferred_element_type=F32)
        gate = g2[:nb] + g2[nb:]
        past = blk_i < i
        gate = jnp.where(past, gate, NEG_INF)
        sel = jnp.zeros((nb, bs), F32)
        for _ in range(MOBA_TOPK):
            best = jnp.max(gate, axis=0, keepdims=True)
            first = jnp.min(jnp.where(gate == best, blk_i, nb), axis=0, keepdims=True)
            pick = blk_i == first
            sel = jnp.where(pick, 1.0, sel)
            gate = jnp.where(pick, PICKED, gate)
        sel_ref[...] = jnp.where(past, sel, 0.0)

        z = jnp.dot(k_ref[0, 0, i], qT, preferred_element_type=F32) - dbias_ref[...]
        m = jnp.max(z, axis=0, keepdims=True)
        p = jnp.exp(z - m)
        l = jnp.sum(p, axis=0, keepdims=True)
        acc = jnp.dot(vT_ref[0, 0, i], p.astype(BF16), preferred_element_type=F32)

        def kv_block(j, mla):
            m, l, acc = mla
            z = jnp.dot(k_ref[0, 0, j], qT, preferred_element_type=F32) - bias_ref[...]
            off = slope * ((i - j) * bs).astype(F32)
            chosen = sel_ref[pl.ds(j, 1), :] > 0.0
            m_new = jnp.where(chosen, jnp.maximum(m, jnp.max(z, axis=0, keepdims=True) - off), m)
            p = jnp.exp(z - jnp.where(chosen, m_new + off, -NEG_INF))
            a = jnp.exp(m - m_new)
            l = a * l + jnp.sum(p, axis=0, keepdims=True)
            acc = a * acc + jnp.dot(vT_ref[0, 0, j], p.astype(BF16), preferred_element_type=F32)
            return m_new, l, acc

        m, l, acc = lax.fori_loop(0, i, kv_block, (m, l, acc))
        o_ref[0, 0, i] = (acc / l).astype(o_ref.dtype)
        return carry

    lax.fori_loop(0, nb, q_block, 0)


def moba_attention(qT, k, vT, slopes):
    b, h, nb, hd, bs = qT.shape
    blk = lambda s: pl.BlockSpec((1, 1) + s, lambda bi, hi, sl: (bi, hi, 0, 0, 0))
    return pl.pallas_call(
        _moba_kernel,
        out_shape=jax.ShapeDtypeStruct(qT.shape, BF16),
        grid_spec=pltpu.PrefetchScalarGridSpec(
            num_scalar_prefetch=1,
            grid=(b, h),
            in_specs=[blk((nb, hd, bs)), blk((nb, bs, hd)), blk((nb, hd, bs))],
            out_specs=blk((nb, hd, bs)),
            scratch_shapes=[pltpu.VMEM((nb, bs), F32),
                            pltpu.VMEM((bs, bs), F32),
                            pltpu.VMEM((bs, bs), F32)]),
        compiler_params=_cparams("parallel", "parallel"),
        name="moba_attention",
    )(slopes, qT, k, vT)


def _swa_kernel(slope_ref, sink_ref, qT_ref, k_ref, vT_ref, o_ref, pbias_ref, cbias_ref):
    nb = qT_ref.shape[2]
    w = k_ref.shape[3]
    gw = qT_ref.shape[4]
    slope = slope_ref[0]
    sink = sink_ref[0]
    key_i = lax.broadcasted_iota(jnp.int32, (w, gw), 0)
    qry_i = lax.broadcasted_iota(jnp.int32, (w, gw), 1) % w
    d_prev = (w + qry_i - key_i).astype(F32)
    d_cur = (qry_i - key_i).astype(F32)
    pbias_ref[...] = jnp.where(d_prev < w, slope * d_prev, -NEG_INF)
    cbias_ref[...] = jnp.where(d_cur >= 0, slope * d_cur, -NEG_INF)

    def q_block(n, carry):
        qT = qT_ref[0, 0, n]
        no_prev = jnp.where(n == 0, -NEG_INF, 0.0)
        zp = jnp.dot(k_ref[0, 0, n], qT, preferred_element_type=F32) - pbias_ref[...] - no_prev
        zc = jnp.dot(k_ref[0, 0, n + 1], qT, preferred_element_type=F32) - cbias_ref[...]
        m = jnp.maximum(jnp.max(zp, axis=0, keepdims=True), jnp.max(zc, axis=0, keepdims=True))
        m = jnp.maximum(m, sink)
        pp = jnp.exp(zp - m)
        pc = jnp.exp(zc - m)
        l = jnp.sum(pp, axis=0, keepdims=True) + jnp.sum(pc, axis=0, keepdims=True) + jnp.exp(sink - m)
        acc = jnp.dot(vT_ref[0, 0, n], pp.astype(BF16), preferred_element_type=F32)
        acc = acc + jnp.dot(vT_ref[0, 0, n + 1], pc.astype(BF16), preferred_element_type=F32)
        o_ref[0, 0, n] = (acc / l).astype(o_ref.dtype)
        return carry

    lax.fori_loop(0, nb, q_block, 0)


def swa_attention(qT, k, vT, slope_lanes, sink_lanes):
    b, kv, nb, hd, gw = qT.shape
    w = k.shape[3]
    idx = lambda bi, ki: (bi, ki, 0, 0, 0)
    lane = pl.BlockSpec((1, 1, gw), lambda bi, ki: (ki, 0, 0))
    return pl.pallas_call(
        _swa_kernel,
        out_shape=jax.ShapeDtypeStruct(qT.shape, BF16),
        grid=(b, kv),
        in_specs=[lane, lane,
                  pl.BlockSpec((1, 1, nb, hd, gw), idx),
                  pl.BlockSpec((1, 1, nb + 1, w, hd), idx),
                  pl.BlockSpec((1, 1, nb + 1, hd, w), idx)],
        out_specs=pl.BlockSpec((1, 1, nb, hd, gw), idx),
        scratch_shapes=[pltpu.VMEM((w, gw), F32), pltpu.VMEM((w, gw), F32)],
        compiler_params=_cparams("parallel", "parallel"),
        name="swa_attention",
    )(slope_lanes, sink_lanes, qT, k, vT)


def _router_gates(x, wr):
    logits = jnp.dot(x, wr, preferred_element_type=F32, precision=lax.Precision.HIGHEST)
    e_i = lax.broadcasted_iota(jnp.int32, logits.shape, 1)
    work = logits
    vals, picks = [], []
    for _ in range(MOE_TOPK):
        best = jnp.max(work, axis=-1, keepdims=True)
        first = jnp.min(jnp.where(work == best, e_i, N_EXPERTS), axis=-1, keepdims=True)
        pick = e_i == first
        vals.append(best)
        picks.append(pick)
        work = jnp.where(pick, PICKED, work)
    ex = [jnp.exp(v - vals[0]) for v in vals]
    denom = functools.reduce(lambda a, c: a + c, ex)
    gates = jnp.zeros(logits.shape, F32)
    for pick, e in zip(picks, ex):
        gates = jnp.where(pick, e / denom, gates)
    return gates


def _ffn_kernel(x_ref, wg_ref, wu_ref, wd_ref, g_ref, b_ref, *rest, routed):
    if routed:
        wr_ref, o_ref, acc_ref, xb_ref, gate_ref = rest
    else:
        o_ref, acc_ref, xb_ref = rest
    e = pl.program_id(1)
    c = pl.program_id(2)
    first = jnp.logical_and(e == 0, c == 0)
    last = jnp.logical_and(e == pl.num_programs(1) - 1, c == pl.num_programs(2) - 1)

    @pl.when(first)
    def _():
        acc_ref[...] = jnp.zeros_like(acc_ref)
        xb_ref[...] = x_ref[...].astype(BF16)
        if routed:
            gate_ref[...] = _router_gates(x_ref[...], wr_ref[...])

    xb = xb_ref[...]
    if routed:
        gates = gate_ref[...]
        e_i = lax.broadcasted_iota(jnp.int32, gates.shape, 1)
        gate_e = jnp.sum(jnp.where(e_i == e, gates, 0.0), axis=-1, keepdims=True)

    def chunk(ci, carry):
        hg = jnp.dot(xb, wg_ref[0, ci], preferred_element_type=F32)
        hu = jnp.dot(xb, wu_ref[0, ci], preferred_element_type=F32)
        hh = hg * jax.nn.sigmoid(hg) * hu
        if routed:
            hh = hh * gate_e
        acc_ref[...] += jnp.dot(hh.astype(BF16), wd_ref[0, ci], preferred_element_type=F32)
        return carry

    lax.fori_loop(0, wg_ref.shape[1], chunk, 0)

    @pl.when(last)
    def _():
        y = DEEPNORM_ALPHA * x_ref[...] + acc_ref[...]
        o_ref[...] = _layer_norm(y, g_ref[...], b_ref[...])


def ffn_ln(x, wg, wu, wd, g, b, w_router=None, tm=512, chunks_per_step=None):
    m, d = x.shape
    ne, nc = wg.shape[0], wg.shape[1]
    cps = chunks_per_step or nc
    tm = min(tm, m)
    routed = w_router is not None
    in_specs = [pl.BlockSpec((tm, d), lambda t, e, c: (t, 0)),
                pl.BlockSpec((1, cps, d, FF_CHUNK), lambda t, e, c: (e, c, 0, 0)),
                pl.BlockSpec((1, cps, d, FF_CHUNK), lambda t, e, c: (e, c, 0, 0)),
                pl.BlockSpec((1, cps, FF_CHUNK, d), lambda t, e, c: (e, c, 0, 0)),
                pl.BlockSpec((1, d), lambda t, e, c: (0, 0)),
                pl.BlockSpec((1, d), lambda t, e, c: (0, 0))]
    args = [x, wg, wu, wd, g.reshape(1, d), b.reshape(1, d)]
    scratch = [pltpu.VMEM((tm, d), F32), pltpu.VMEM((tm, d), BF16)]
    if routed:
        in_specs.append(pl.BlockSpec((d, ne), lambda t, e, c: (0, 0)))
        args.append(w_router)
        scratch.append(pltpu.VMEM((tm, ne), F32))
    return pl.pallas_call(
        functools.partial(_ffn_kernel, routed=routed),
        out_shape=jax.ShapeDtypeStruct((m, d), F32),
        grid=(m // tm, ne, nc // cps),
        in_specs=in_specs,
        out_specs=pl.BlockSpec((tm, d), lambda t, e, c: (t, 0)),
        scratch_shapes=scratch,
        compiler_params=_cparams("parallel", "arbitrary", "arbitrary"),
        name="moe_ffn_ln" if routed else "ffn_ln",
    )(*args)


def _alibi_slopes(n):
    return jnp.exp2(-8.0 * jnp.arange(1, n + 1, dtype=F32) / n)


def _chunk_cols(w):
    *lead, d, f = w.shape
    w = w.reshape(*lead, d, f // FF_CHUNK, FF_CHUNK)
    return jnp.swapaxes(w, -3, -2).astype(BF16)


def _chunk_rows(w):
    *lead, f, d = w.shape
    return w.reshape(*lead, f // FF_CHUNK, FF_CHUNK, d).astype(BF16)


def kernel(x, w_qkv_a, w_o_a, w_kv_shared, w_q_b, w_o_b, sinks_b, w_gate_d, w_up_d, w_down_d,
           w_router, w_gate_e, w_up_e, w_down_e, ln_gain, ln_bias):
    B, T, D = x.shape
    hd = HEAD_DIM
    H = D // hd
    N = B * T
    xs = x.reshape(N, D)
    q_scale = hd ** -0.5

    HA = w_qkv_a.shape[-1] // 3
    col_scale = jnp.concatenate([jnp.full((HA,), q_scale, F32), jnp.ones((2 * HA,), F32)])
    qkv = matmul(xs, (w_qkv_a[0] * col_scale).astype(BF16), BF16)
    bs = MOBA_BLOCK
    nb = T // bs
    q, k, v = (qkv[:, i * HA:(i + 1) * HA].reshape(B, nb, bs, H, hd) for i in range(3))
    qT = q.transpose(0, 3, 1, 4, 2)
    kb = k.transpose(0, 3, 1, 2, 4)
    vT = v.transpose(0, 3, 1, 4, 2)
    oT = moba_attention(qT, kb, vT, _alibi_slopes(H))
    o = oT.transpose(0, 2, 4, 1, 3).reshape(N, HA)
    xs = proj_ln(o, w_o_a[0].astype(BF16), xs, ln_gain[0, 0], ln_bias[0, 0])

    xs = ffn_ln(xs, _chunk_cols(w_gate_d[0])[None], _chunk_cols(w_up_d[0])[None],
                _chunk_rows(w_down_d[0])[None], ln_gain[0, 1], ln_bias[0, 1])

    KV = N_KV_B
    G = H // KV
    W = WINDOW
    nw = T // W
    w_qkv_b = jnp.concatenate([w_q_b[0] * q_scale, w_kv_shared], axis=1).astype(BF16)
    qkv = matmul(xs, w_qkv_b, BF16, tn=w_qkv_b.shape[1])
    q = qkv[:, :D].reshape(B, nw, W, KV, G, hd)
    qT = q.transpose(0, 3, 1, 5, 4, 2).reshape(B, KV, nw, hd, G * W)
    k = qkv[:, D:D + KV * hd].reshape(B, nw, W, KV, hd).transpose(0, 3, 1, 2, 4)
    v = qkv[:, D + KV * hd:].reshape(B, nw, W, KV, hd).transpose(0, 3, 1, 4, 2)
    kb = jnp.pad(k, ((0, 0), (0, 0), (1, 0), (0, 0), (0, 0)))
    vT = jnp.pad(v, ((0, 0), (0, 0), (1, 0), (0, 0), (0, 0)))
    slope_lanes = jnp.repeat(_alibi_slopes(H).reshape(KV, G), W, axis=1).reshape(KV, 1, G * W)
    sink_lanes = jnp.repeat(sinks_b[0].astype(F32).reshape(KV, G), W, axis=1).reshape(KV, 1, G * W)
    oT = swa_attention(qT, kb, vT, slope_lanes, sink_lanes)
    o = oT.reshape(B, KV, nw, hd, G, W).transpose(0, 2, 5, 1, 4, 3).reshape(N, D)
    xs = proj_ln(o, w_o_b[0].astype(BF16), xs, ln_gain[1, 0], ln_bias[1, 0])

    xs = ffn_ln(xs, _chunk_cols(w_gate_e[0]), _chunk_cols(w_up_e[0]), _chunk_rows(w_down_e[0]),
                ln_gain[1, 1], ln_bias[1, 1], w_router=w_router[0], chunks_per_step=7)
    return xs.reshape(B, T, D)
```

```python
import functools

import jax
import jax.numpy as jnp
from jax import lax
from jax.experimental import pallas as pl
from jax.experimental.pallas import tpu as pltpu

HEAD_DIM = 64
MOBA_BLOCK = 256
MOBA_TOPK = 3
N_KV_B = 2
WINDOW = 128
N_EXPERTS = 8
MOE_TOPK = 2
DEPTH = 2
DEEPNORM_ALPHA = (2 * DEPTH) ** 0.25
LN_EPS = 1e-5
NEG_INF = -1e30
PICKED = -3e38
FF_CHUNK = 256
VMEM_LIMIT = 56 * 1024 * 1024

F32 = jnp.float32
BF16 = jnp.bfloat16


def _cparams(*sem):
    return pltpu.CompilerParams(dimension_semantics=sem, vmem_limit_bytes=VMEM_LIMIT)


def _mm_kernel(a_ref, w_ref, o_ref):
    a = a_ref[...].astype(BF16)
    o_ref[...] = jnp.dot(a, w_ref[...], preferred_element_type=F32).astype(o_ref.dtype)


def matmul(a, w, out_dtype, tm=1024, tn=1024):
    m, k = a.shape
    n = w.shape[1]
    tm, tn = min(tm, m), min(tn, n)
    return pl.pallas_call(
        _mm_kernel,
        out_shape=jax.ShapeDtypeStruct((m, n), out_dtype),
        grid=(m // tm, n // tn),
        in_specs=[pl.BlockSpec((tm, k), lambda i, j: (i, 0)),
                  pl.BlockSpec((k, tn), lambda i, j: (0, j))],
        out_specs=pl.BlockSpec((tm, tn), lambda i, j: (i, j)),
        compiler_params=_cparams("parallel", "arbitrary"),
        name="proj",
    )(a, w)


def _layer_norm(y, g, b):
    mu = jnp.mean(y, axis=-1, keepdims=True)
    yc = y - mu
    var = jnp.mean(yc * yc, axis=-1, keepdims=True)
    return yc * lax.rsqrt(var + LN_EPS) * g + b


def _proj_ln_kernel(a_ref, w_ref, x_ref, g_ref, b_ref, o_ref):
    mix = jnp.dot(a_ref[...], w_ref[...], preferred_element_type=F32)
    y = DEEPNORM_ALPHA * x_ref[...] + mix
    o_ref[...] = _layer_norm(y, g_ref[...], b_ref[...])


def proj_ln(a, w, x, g, b, tm=512):
    m, k = a.shape
    d = w.shape[1]
    tm = min(tm, m)
    return pl.pallas_call(
        _proj_ln_kernel,
        out_shape=jax.ShapeDtypeStruct((m, d), F32),
        grid=(m // tm,),
        in_specs=[pl.BlockSpec((tm, k), lambda i: (i, 0)),
                  pl.BlockSpec((k, d), lambda i: (0, 0)),
                  pl.BlockSpec((tm, d), lambda i: (i, 0)),
                  pl.BlockSpec((1, d), lambda i: (0, 0)),
                  pl.BlockSpec((1, d), lambda i: (0, 0))],
        out_specs=pl.BlockSpec((tm, d), lambda i: (i, 0)),
        compiler_params=_cparams("parallel"),
        name="proj_ln",
    )(a, w, x, g.reshape(1, d), b.reshape(1, d))


def _moba_kernel(slope_ref, qblk_ref, kblk_ref, qT_ref, k_ref, vT_ref, o_ref,
                 sel_ref, bias_ref, z_ref, p_ref, acc_ref, st_ref):
    hg, nb, bs = k_ref.shape[1], k_ref.shape[2], k_ref.shape[3]
    hd = k_ref.shape[4]
    n_steps = nb * (nb + 1) // 2
    heads = range(hg)
    slopes = [slope_ref[pl.program_id(1) * hg + g] for g in heads]

    key_i = lax.broadcasted_iota(jnp.int32, (bs, bs), 0)
    qry_i = lax.broadcasted_iota(jnp.int32, (bs, bs), 1)
    rel = (qry_i - key_i).astype(F32)
    blk_i = lax.broadcasted_iota(jnp.int32, (nb, bs), 0)
    km2 = []
    for g in heads:
        bias = slopes[g] * rel
        bias_ref[g, 0] = bias
        bias_ref[g, 1] = jnp.where(rel >= 0, bias, -NEG_INF)
        kmean = jnp.sum(k_ref[0, g].astype(F32), axis=1) * (1.0 / bs)
        km_hi = kmean.astype(BF16)
        km_lo = (kmean - km_hi.astype(F32)).astype(BF16)
        km2.append(jnp.concatenate([km_hi, km_lo], axis=0))

    def select(i, carry):
        past = blk_i < i
        own = jnp.where(blk_i == i, 1.0, 0.0)
        for g in heads:
            g2 = jnp.dot(km2[g], qT_ref[0, g, i], preferred_element_type=F32)
            gate = jnp.where(past, g2[:nb] + g2[nb:], NEG_INF)
            sel = jnp.zeros((nb, bs), F32)
            for _ in range(MOBA_TOPK):
                best = jnp.max(gate, axis=0, keepdims=True)
                first = jnp.min(jnp.where(gate == best, blk_i, nb), axis=0, keepdims=True)
                pick = blk_i == first
                sel = jnp.where(pick, 1.0, sel)
                gate = jnp.where(pick, PICKED, gate)
            sel_ref[g, i] = jnp.where(past, sel, own)
        return carry

    lax.fori_loop(0, nb, select, 0)

    ROW_M, ROW_L, ROW_A, ROW_PSUM = 0, 1, 2, 4
    z_ref[...] = jnp.zeros_like(z_ref)
    p_ref[...] = jnp.zeros_like(p_ref)
    acc_ref[...] = jnp.zeros_like(acc_ref)
    row_i = lax.broadcasted_iota(jnp.int32, (8, bs), 0)
    st0 = jnp.where(row_i == ROW_M, NEG_INF, jnp.where((row_i == ROW_PSUM) | (row_i == ROW_PSUM + 1), 0.0, 1.0))
    for g in heads:
        st_ref[g] = st0

    def step(s, par):
        sa = jnp.minimum(s, n_steps - 1)
        sb = jnp.clip(s - 1, 0, n_steps - 1)
        sc = jnp.clip(s - 2, 0, n_steps - 1)
        qa, ka = qblk_ref[sa], kblk_ref[sa]
        qb, kb = qblk_ref[sb], kblk_ref[sb]
        qc, kc = qblk_ref[sc], kblk_ref[sc]
        own_a = (qa == ka).astype(jnp.int32)
        own = qb == kb
        for g in heads:
            scores = jnp.dot(k_ref[0, g, ka], qT_ref[0, g, qa], preferred_element_type=F32)
            z_ref[par, g] = scores - bias_ref[g, own_a]

            off = slopes[g] * ((qb - kb) * bs).astype(F32)
            chosen = sel_ref[g, qb, pl.ds(kb, 1), :] > 0.0
            m_in = jnp.where(own, NEG_INF, st_ref[g, pl.ds(ROW_M, 1), :])
            tile_max = jnp.max(z_ref[1 - par, g], axis=0, keepdims=True)
            m_new = jnp.where(chosen, jnp.maximum(m_in, tile_max - off), m_in)
            p_f = jnp.exp(z_ref[1 - par, g] - jnp.where(chosen, m_new + off, -NEG_INF))
            p_ref[1 - par, g] = p_f.astype(BF16)
            st_ref[g, pl.ds(ROW_M, 1), :] = m_new
            st_ref[g, pl.ds(ROW_A + 1 - par, 1), :] = jnp.exp(m_in - m_new)
            st_ref[g, pl.ds(ROW_PSUM + 1 - par, 1), :] = jnp.sum(p_f, axis=0, keepdims=True)

            a = st_ref[g, pl.ds(ROW_A + par, 1), :]
            l_new = a * st_ref[g, pl.ds(ROW_L, 1), :] + st_ref[g, pl.ds(ROW_PSUM + par, 1), :]
            st_ref[g, pl.ds(ROW_L, 1), :] = l_new
            acc_new = a * acc_ref[g] + jnp.dot(vT_ref[0, g, kc], p_ref[par, g], preferred_element_type=F32)
            acc_ref[g] = acc_new
            o_ref[0, g, qc] = (acc_new / l_new).astype(o_ref.dtype)

    def two_steps(t, carry):
        step(2 * t, 0)
        step(2 * t + 1, 1)
        return carry

    lax.fori_loop(0, (n_steps + 2) // 2, two_steps, 0)


def moba_attention(qT, k, vT, slopes, heads_per_step=4):
    b, h, nb, hd, bs = qT.shape
    hg = heads_per_step
    steps = [(i, i if t == 0 else t - 1) for i in range(nb) for t in range(i + 1)]
    qblk = jnp.asarray([s[0] for s in steps], jnp.int32)
    kblk = jnp.asarray([s[1] for s in steps], jnp.int32)
    blk = lambda s: pl.BlockSpec((1, hg) + s, lambda bi, hi, *_: (bi, hi, 0, 0, 0))
    return pl.pallas_call(
        _moba_kernel,
        out_shape=jax.ShapeDtypeStruct(qT.shape, BF16),
        grid_spec=pltpu.PrefetchScalarGridSpec(
            num_scalar_prefetch=3,
            grid=(b, h // hg),
            in_specs=[blk((nb, hd, bs)), blk((nb, bs, hd)), blk((nb, hd, bs))],
            out_specs=blk((nb, hd, bs)),
            scratch_shapes=[pltpu.VMEM((hg, nb, nb, bs), F32),
                            pltpu.VMEM((hg, 2, bs, bs), F32),
                            pltpu.VMEM((2, hg, bs, bs), F32),
                            pltpu.VMEM((2, hg, bs, bs), BF16),
                            pltpu.VMEM((hg, hd, bs), F32),
                            pltpu.VMEM((hg, 8, bs), F32)]),
        compiler_params=_cparams("parallel", "parallel"),
        name="moba_attention",
    )(slopes, qblk, kblk, qT, k, vT)


def _swa_kernel(slope_ref, sink_ref, qT_ref, k_ref, vT_ref, o_ref, pbias_ref, cbias_ref):
    nb = qT_ref.shape[2]
    w = k_ref.shape[3]
    gw = qT_ref.shape[4]
    slope = slope_ref[0]
    sink = sink_ref[0]
    key_i = lax.broadcasted_iota(jnp.int32, (w, gw), 0)
    qry_i = lax.broadcasted_iota(jnp.int32, (w, gw), 1) % w
    d_prev = (w + qry_i - key_i).astype(F32)
    d_cur = (qry_i - key_i).astype(F32)
    pbias_ref[...] = jnp.where(d_prev < w, slope * d_prev, -NEG_INF)
    cbias_ref[...] = jnp.where(d_cur >= 0, slope * d_cur, -NEG_INF)

    def q_block(n, carry):
        qT = qT_ref[0, 0, n]
        no_prev = jnp.where(n == 0, -NEG_INF, 0.0)
        zp = jnp.dot(k_ref[0, 0, n], qT, preferred_element_type=F32) - pbias_ref[...] - no_prev
        zc = jnp.dot(k_ref[0, 0, n + 1], qT, preferred_element_type=F32) - cbias_ref[...]
        m = jnp.maximum(jnp.max(zp, axis=0, keepdims=True), jnp.max(zc, axis=0, keepdims=True))
        m = jnp.maximum(m, sink)
        pp = jnp.exp(zp - m)
        pc = jnp.exp(zc - m)
        l = jnp.sum(pp, axis=0, keepdims=True) + jnp.sum(pc, axis=0, keepdims=True) + jnp.exp(sink - m)
        acc = jnp.dot(vT_ref[0, 0, n], pp.astype(BF16), preferred_element_type=F32)
        acc = acc + jnp.dot(vT_ref[0, 0, n + 1], pc.astype(BF16), preferred_element_type=F32)
        o_ref[0, 0, n] = (acc / l).astype(o_ref.dtype)
        return carry

    lax.fori_loop(0, nb, q_block, 0)


def swa_attention(qT, k, vT, slope_lanes, sink_lanes):
    b, kv, nb, hd, gw = qT.shape
    w = k.shape[3]
    idx = lambda bi, ki: (bi, ki, 0, 0, 0)
    lane = pl.BlockSpec((1, 1, gw), lambda bi, ki: (ki, 0, 0))
    return pl.pallas_call(
        _swa_kernel,
        out_shape=jax.ShapeDtypeStruct(qT.shape, BF16),
        grid=(b, kv),
        in_specs=[lane, lane,
                  pl.BlockSpec((1, 1, nb, hd, gw), idx),
                  pl.BlockSpec((1, 1, nb + 1, w, hd), idx),
                  pl.BlockSpec((1, 1, nb + 1, hd, w), idx)],
        out_specs=pl.BlockSpec((1, 1, nb, hd, gw), idx),
        scratch_shapes=[pltpu.VMEM((w, gw), F32), pltpu.VMEM((w, gw), F32)],
        compiler_params=_cparams("parallel", "parallel"),
        name="swa_attention",
    )(slope_lanes, sink_lanes, qT, k, vT)


def _swiglu_chunks(xb, wg_ref, wu_ref, wd_ref):
    def chunk(ci, acc):
        hg = jnp.dot(xb, wg_ref[ci], preferred_element_type=F32)
        hu = jnp.dot(xb, wu_ref[ci], preferred_element_type=F32)
        hh = hg * jax.nn.sigmoid(hg) * hu
        return acc + jnp.dot(hh.astype(BF16), wd_ref[ci], preferred_element_type=F32)

    return lax.fori_loop(0, wg_ref.shape[0], chunk, jnp.zeros((xb.shape[0], wd_ref.shape[2]), F32))


def _ffn_kernel(x_ref, wg_ref, wu_ref, wd_ref, g_ref, b_ref, o_ref):
    ffn = _swiglu_chunks(x_ref[...].astype(BF16), wg_ref, wu_ref, wd_ref)
    y = DEEPNORM_ALPHA * x_ref[...] + ffn
    o_ref[...] = _layer_norm(y, g_ref[...], b_ref[...])


def ffn_ln(x, wg, wu, wd, g, b, tm=512):
    m, d = x.shape
    nc = wg.shape[0]
    tm = min(tm, m)
    whole = lambda s: pl.BlockSpec(s, lambda t: (0,) * len(s))
    return pl.pallas_call(
        _ffn_kernel,
        out_shape=jax.ShapeDtypeStruct((m, d), F32),
        grid=(m // tm,),
        in_specs=[pl.BlockSpec((tm, d), lambda t: (t, 0)),
                  whole((nc, d, FF_CHUNK)), whole((nc, d, FF_CHUNK)), whole((nc, FF_CHUNK, d)),
                  whole((1, d)), whole((1, d))],
        out_specs=pl.BlockSpec((tm, d), lambda t: (t, 0)),
        compiler_params=_cparams("parallel"),
        name="ffn_ln",
    )(x, wg, wu, wd, g.reshape(1, d), b.reshape(1, d))


def _router_kernel(x_ref, wrT_ref, gate_ref, rank_ref, count_ref):
    tt = x_ref.shape[0]
    logits = lax.dot_general(wrT_ref[...], x_ref[...], (((1,), (1,)), ((), ())),
                             preferred_element_type=F32, precision=lax.Precision.HIGHEST)
    e_i = lax.broadcasted_iota(jnp.int32, logits.shape, 0)
    work = logits
    vals, picks = [], []
    for _ in range(MOE_TOPK):
        best = jnp.max(work, axis=0, keepdims=True)
        first = jnp.min(jnp.where(work == best, e_i, N_EXPERTS), axis=0, keepdims=True)
        pick = e_i == first
        vals.append(best)
        picks.append(pick)
        work = jnp.where(pick, PICKED, work)
    ex = [jnp.exp(v - vals[0]) for v in vals]
    denom = functools.reduce(lambda a, c: a + c, ex)
    gates = jnp.zeros(logits.shape, F32)
    sel = jnp.zeros(logits.shape, F32)
    for pick, e in zip(picks, ex):
        gates = jnp.where(pick, e / denom, gates)
        sel = jnp.where(pick, 1.0, sel)
    gate_ref[...] = gates
    before = (lax.broadcasted_iota(jnp.int32, (tt, tt), 0) < lax.broadcasted_iota(jnp.int32, (tt, tt), 1))
    rank = jnp.dot(sel.astype(BF16), jnp.where(before, 1.0, 0.0).astype(BF16), preferred_element_type=F32)
    rank_ref[...] = jnp.where(sel > 0.0, rank, -1.0)
    count_ref[0] = jnp.broadcast_to(jnp.sum(sel, axis=1, keepdims=True), count_ref.shape[1:])


def router(x, wrT, tt):
    n, d = x.shape
    ne = wrT.shape[0]
    return pl.pallas_call(
        _router_kernel,
        out_shape=(jax.ShapeDtypeStruct((ne, n), F32), jax.ShapeDtypeStruct((ne, n), F32),
                   jax.ShapeDtypeStruct((n // tt, ne, 128), F32)),
        grid=(n // tt,),
        in_specs=[pl.BlockSpec((tt, d), lambda t: (t, 0)), pl.BlockSpec((ne, d), lambda t: (0, 0))],
        out_specs=(pl.BlockSpec((ne, tt), lambda t: (0, t)), pl.BlockSpec((ne, tt), lambda t: (0, t)),
                   pl.BlockSpec((1, ne, 128), lambda t: (t, 0, 0))),
        compiler_params=_cparams("parallel"),
        name="router",
    )(x, wrT)


MOE_ROWS = 128
MOE_TOKEN_TILE = 1024


def _moe_kernel(count_ref, x_ref, gate_ref, rank_ref, wg_ref, wu_ref, wd_ref, g_ref, b_ref, o_ref,
                xb_ref, xe_ref, ye_ref):
    t, e, half = pl.program_id(0), pl.program_id(1), pl.program_id(2)
    ne, nh = pl.num_programs(1), pl.num_programs(2)
    tt = x_ref.shape[0]
    rows = MOE_ROWS
    n_chunks = (count_ref[t * ne + e] + rows - 1) // rows
    rank_row = rank_ref[pl.ds(e, 1), :]
    row_i = lax.broadcasted_iota(jnp.int32, (rows, tt), 0).astype(F32)

    def one_hot(c):
        return rank_row == row_i + (c * rows).astype(F32)

    @pl.when(jnp.logical_and(e == 0, half == 0))
    def _():
        o_ref[...] = jnp.zeros_like(o_ref)
        xb_ref[...] = x_ref[...].astype(BF16)

    @pl.when(half == 0)
    def _():
        def gather(c, carry):
            hot = jnp.where(one_hot(c), 1.0, 0.0).astype(BF16)
            r0 = pl.multiple_of(c * rows, rows)
            xe_ref[pl.ds(r0, rows), :] = jnp.dot(hot, xb_ref[...], preferred_element_type=F32).astype(BF16)
            ye_ref[pl.ds(r0, rows), :] = jnp.zeros((rows, ye_ref.shape[1]), F32)
            return carry
        lax.fori_loop(0, n_chunks, gather, 0)

    def expert(c, carry):
        r0 = pl.multiple_of(c * rows, rows)
        ye = _swiglu_chunks(xe_ref[pl.ds(r0, rows), :], wg_ref.at[0], wu_ref.at[0], wd_ref.at[0])
        ye_ref[pl.ds(r0, rows), :] += ye
        return carry
    lax.fori_loop(0, n_chunks, expert, 0)

    @pl.when(half == nh - 1)
    def _():
        gate_row = gate_ref[pl.ds(e, 1), :]
        def scatter(c, carry):
            hot = one_hot(c)
            r0 = pl.multiple_of(c * rows, rows)
            w_col = jnp.sum(jnp.where(hot, gate_row, 0.0), axis=1, keepdims=True)
            yw = (ye_ref[pl.ds(r0, rows), :] * w_col).astype(BF16)
            o_ref[...] += lax.dot_general(jnp.where(hot, 1.0, 0.0).astype(BF16), yw,
                                          (((0,), (0,)), ((), ())), preferred_element_type=F32)
            return carry
        lax.fori_loop(0, n_chunks, scatter, 0)

    @pl.when(jnp.logical_and(e == ne - 1, half == nh - 1))
    def _():
        y = DEEPNORM_ALPHA * x_ref[...] + o_ref[...]
        o_ref[...] = _layer_norm(y, g_ref[...], b_ref[...])


def moe_ffn_ln(x, gate, rank, count, wg, wu, wd, g, b, tt, halves=2):
    n, d = x.shape
    ne, nc = wg.shape[0], wg.shape[1]
    cps = nc // halves
    wspec = lambda s: pl.BlockSpec((1, cps) + s, lambda t, e, h, *_: (e, h, 0, 0))
    row = pl.BlockSpec((1, d), lambda t, e, h, *_: (0, 0))
    return pl.pallas_call(
        _moe_kernel,
        out_shape=jax.ShapeDtypeStruct((n, d), F32),
        grid_spec=pltpu.PrefetchScalarGridSpec(
            num_scalar_prefetch=1,
            grid=(n // tt, ne, halves),
            in_specs=[pl.BlockSpec((tt, d), lambda t, e, h, *_: (t, 0)),
                      pl.BlockSpec((ne, tt), lambda t, e, h, *_: (0, t)),
                      pl.BlockSpec((ne, tt), lambda t, e, h, *_: (0, t)),
                      wspec((d, FF_CHUNK)), wspec((d, FF_CHUNK)), wspec((FF_CHUNK, d)), row, row],
            out_specs=pl.BlockSpec((tt, d), lambda t, e, h, *_: (t, 0)),
            scratch_shapes=[pltpu.VMEM((tt, d), BF16), pltpu.VMEM((tt, d), BF16), pltpu.VMEM((tt, d), F32)]),
        compiler_params=_cparams("parallel", "arbitrary", "arbitrary"),
        name="moe_ffn_ln",
    )(count, x, gate, rank, wg, wu, wd, g.reshape(1, d), b.reshape(1, d))


def _alibi_slopes(n):
    return jnp.exp2(-8.0 * jnp.arange(1, n + 1, dtype=F32) / n)


def _chunk_cols(w):
    *lead, d, f = w.shape
    w = w.reshape(*lead, d, f // FF_CHUNK, FF_CHUNK)
    return jnp.swapaxes(w, -3, -2).astype(BF16)


def _chunk_rows(w):
    *lead, f, d = w.shape
    return w.reshape(*lead, f // FF_CHUNK, FF_CHUNK, d).astype(BF16)


def kernel(x, w_qkv_a, w_o_a, w_kv_shared, w_q_b, w_o_b, sinks_b, w_gate_d, w_up_d, w_down_d,
           w_router, w_gate_e, w_up_e, w_down_e, ln_gain, ln_bias):
    B, T, D = x.shape
    hd = HEAD_DIM
    H = D // hd
    N = B * T
    xs = x.reshape(N, D)
    q_scale = hd ** -0.5

    HA = w_qkv_a.shape[-1] // 3
    col_scale = jnp.concatenate([jnp.full((HA,), q_scale, F32), jnp.ones((2 * HA,), F32)])
    qkv = matmul(xs, (w_qkv_a[0] * col_scale).astype(BF16), BF16)
    bs = MOBA_BLOCK
    nb = T // bs
    q, k, v = (qkv[:, i * HA:(i + 1) * HA].reshape(B, nb, bs, H, hd) for i in range(3))
    qT = q.transpose(0, 3, 1, 4, 2)
    kb = k.transpose(0, 3, 1, 2, 4)
    vT = v.transpose(0, 3, 1, 4, 2)
    oT = moba_attention(qT, kb, vT, _alibi_slopes(H))
    o = oT.transpose(0, 2, 4, 1, 3).reshape(N, HA)
    xs = proj_ln(o, w_o_a[0].astype(BF16), xs, ln_gain[0, 0], ln_bias[0, 0])

    xs = ffn_ln(xs, _chunk_cols(w_gate_d[0]), _chunk_cols(w_up_d[0]), _chunk_rows(w_down_d[0]),
                ln_gain[0, 1], ln_bias[0, 1])

    KV = N_KV_B
    G = H // KV
    W = WINDOW
    nw = T // W
    w_qkv_b = jnp.concatenate([w_q_b[0] * q_scale, w_kv_shared], axis=1).astype(BF16)
    qkv = matmul(xs, w_qkv_b, BF16, tn=w_qkv_b.shape[1])
    q = qkv[:, :D].reshape(B, nw, W, KV, G, hd)
    qT = q.transpose(0, 3, 1, 5, 4, 2).reshape(B, KV, nw, hd, G * W)
    k = qkv[:, D:D + KV * hd].reshape(B, nw, W, KV, hd).transpose(0, 3, 1, 2, 4)
    v = qkv[:, D + KV * hd:].reshape(B, nw, W, KV, hd).transpose(0, 3, 1, 4, 2)
    kb = jnp.pad(k, ((0, 0), (0, 0), (1, 0), (0, 0), (0, 0)))
    vT = jnp.pad(v, ((0, 0), (0, 0), (1, 0), (0, 0), (0, 0)))
    slope_lanes = jnp.repeat(_alibi_slopes(H).reshape(KV, G), W, axis=1).reshape(KV, 1, G * W)
    sink_lanes = jnp.repeat(sinks_b[0].astype(F32).reshape(KV, G), W, axis=1).reshape(KV, 1, G * W)
    oT = swa_attention(qT, kb, vT, slope_lanes, sink_lanes)
    o = oT.reshape(B, KV, nw, hd, G, W).transpose(0, 2, 5, 1, 4, 3).reshape(N, D)
    xs = proj_ln(o, w_o_b[0].astype(BF16), xs, ln_gain[1, 0], ln_bias[1, 0])

    tt = min(MOE_TOKEN_TILE, N)
    gate, rank, count = router(xs, w_router[0].T, tt)
    count = count[:, :, 0].astype(jnp.int32).reshape(-1)
    xs = moe_ffn_ln(xs, gate, rank, count, _chunk_cols(w_gate_e[0]), _chunk_cols(w_up_e[0]),
                    _chunk_rows(w_down_e[0]), ln_gain[1, 1], ln_bias[1, 1], tt)
    return xs.reshape(B, T, D)
```

```python
import functools

import jax
import jax.numpy as jnp
from jax import lax
from jax.experimental import pallas as pl
from jax.experimental.pallas import tpu as pltpu

HEAD_DIM = 64
MOBA_BLOCK = 256
MOBA_TOPK = 3
N_KV_B = 2
WINDOW = 128
N_EXPERTS = 8
MOE_TOPK = 2
DEPTH = 2
DEEPNORM_ALPHA = (2 * DEPTH) ** 0.25
LN_EPS = 1e-5
NEG_INF = -1e30
PICKED = -3e38
VMEM_LIMIT = 56 * 1024 * 1024

F32 = jnp.float32
BF16 = jnp.bfloat16


def _cparams(*sem):
    return pltpu.CompilerParams(dimension_semantics=sem, vmem_limit_bytes=VMEM_LIMIT)


def _mm_kernel(a_ref, w_ref, o_ref):
    a = a_ref[...].astype(BF16)
    o_ref[...] = jnp.dot(a, w_ref[...], preferred_element_type=F32).astype(o_ref.dtype)


def matmul(a, w, out_dtype, tm=1024, tn=1024):
    m, k = a.shape
    n = w.shape[1]
    tm, tn = min(tm, m), min(tn, n)
    return pl.pallas_call(
        _mm_kernel,
        out_shape=jax.ShapeDtypeStruct((m, n), out_dtype),
        grid=(m // tm, n // tn),
        in_specs=[pl.BlockSpec((tm, k), lambda i, j: (i, 0)),
                  pl.BlockSpec((k, tn), lambda i, j: (0, j))],
        out_specs=pl.BlockSpec((tm, tn), lambda i, j: (i, j)),
        compiler_params=_cparams("parallel", "arbitrary"),
        name="proj",
    )(a, w)


def _layer_norm(y, g, b):
    mu = jnp.mean(y, axis=-1, keepdims=True)
    yc = y - mu
    var = jnp.mean(yc * yc, axis=-1, keepdims=True)
    return yc * lax.rsqrt(var + LN_EPS) * g + b


def _proj_ln_kernel(a_ref, w_ref, x_ref, g_ref, b_ref, o_ref):
    mix = jnp.dot(a_ref[...], w_ref[...], preferred_element_type=F32)
    y = DEEPNORM_ALPHA * x_ref[...] + mix
    o_ref[...] = _layer_norm(y, g_ref[...], b_ref[...])


def proj_ln(a, w, x, g, b, tm=512):
    m, k = a.shape
    d = w.shape[1]
    tm = min(tm, m)
    return pl.pallas_call(
        _proj_ln_kernel,
        out_shape=jax.ShapeDtypeStruct((m, d), F32),
        grid=(m // tm,),
        in_specs=[pl.BlockSpec((tm, k), lambda i: (i, 0)),
                  pl.BlockSpec((k, d), lambda i: (0, 0)),
                  pl.BlockSpec((tm, d), lambda i: (i, 0)),
                  pl.BlockSpec((1, d), lambda i: (0, 0)),
                  pl.BlockSpec((1, d), lambda i: (0, 0))],
        out_specs=pl.BlockSpec((tm, d), lambda i: (i, 0)),
        compiler_params=_cparams("parallel"),
        name="proj_ln",
    )(a, w, x, g.reshape(1, d), b.reshape(1, d))


def _moba_kernel(slope_ref, qblk_ref, kblk_ref, qT_ref, k_ref, vT_ref, o_ref,
                 sel_ref, bias_ref, z_ref, p_ref, acc_ref, st_ref):
    hg, nb, bs = k_ref.shape[1], k_ref.shape[2], k_ref.shape[3]
    hd = k_ref.shape[4]
    n_steps = nb * (nb + 1) // 2
    heads = range(hg)
    slopes = [slope_ref[pl.program_id(1) * hg + g] for g in heads]

    key_i = lax.broadcasted_iota(jnp.int32, (bs, bs), 0)
    qry_i = lax.broadcasted_iota(jnp.int32, (bs, bs), 1)
    rel = (qry_i - key_i).astype(F32)
    blk_i = lax.broadcasted_iota(jnp.int32, (nb, bs), 0)
    km2 = []
    for g in heads:
        bias = slopes[g] * rel
        bias_ref[g, 0] = bias
        bias_ref[g, 1] = jnp.where(rel >= 0, bias, -NEG_INF)
        kmean = jnp.sum(k_ref[0, g].astype(F32), axis=1) * (1.0 / bs)
        km_hi = kmean.astype(BF16)
        km_lo = (kmean - km_hi.astype(F32)).astype(BF16)
        km2.append(jnp.concatenate([km_hi, km_lo], axis=0))

    def select(i, carry):
        past = blk_i < i
        own = jnp.where(blk_i == i, 1.0, 0.0)
        for g in heads:
            g2 = jnp.dot(km2[g], qT_ref[0, g, i], preferred_element_type=F32)
            gate = jnp.where(past, g2[:nb] + g2[nb:], NEG_INF)
            sel = jnp.zeros((nb, bs), F32)
            for _ in range(MOBA_TOPK):
                best = jnp.max(gate, axis=0, keepdims=True)
                first = jnp.min(jnp.where(gate == best, blk_i, nb), axis=0, keepdims=True)
                pick = blk_i == first
                sel = jnp.where(pick, 1.0, sel)
                gate = jnp.where(pick, PICKED, gate)
            sel_ref[g, i] = jnp.where(past, sel, own)
        return carry

    lax.fori_loop(0, nb, select, 0)

    ROW_M, ROW_L, ROW_A, ROW_PSUM = 0, 1, 2, 4
    z_ref[...] = jnp.zeros_like(z_ref)
    p_ref[...] = jnp.zeros_like(p_ref)
    acc_ref[...] = jnp.zeros_like(acc_ref)
    row_i = lax.broadcasted_iota(jnp.int32, (8, bs), 0)
    st0 = jnp.where(row_i == ROW_M, NEG_INF, jnp.where((row_i == ROW_PSUM) | (row_i == ROW_PSUM + 1), 0.0, 1.0))
    for g in heads:
        st_ref[g] = st0

    def step(s, par):
        sa = jnp.minimum(s, n_steps - 1)
        sb = jnp.clip(s - 1, 0, n_steps - 1)
        sc = jnp.clip(s - 2, 0, n_steps - 1)
        qa, ka = qblk_ref[sa], kblk_ref[sa]
        qb, kb = qblk_ref[sb], kblk_ref[sb]
        qc, kc = qblk_ref[sc], kblk_ref[sc]
        own_a = (qa == ka).astype(jnp.int32)
        own = qb == kb
        for g in heads:
            scores = jnp.dot(k_ref[0, g, ka], qT_ref[0, g, qa], preferred_element_type=F32)
            z_ref[par, g] = scores - bias_ref[g, own_a]

            off = slopes[g] * lax.convert_element_type((qb - kb) * bs, F32)
            chosen = sel_ref[g, qb, pl.ds(kb, 1), :] > 0.0
            m_in = jnp.where(own, NEG_INF, st_ref[g, pl.ds(ROW_M, 1), :])
            tile_max = jnp.max(z_ref[1 - par, g], axis=0, keepdims=True)
            m_new = jnp.where(chosen, jnp.maximum(m_in, tile_max - off), m_in)
            p_f = jnp.exp(z_ref[1 - par, g] - jnp.where(chosen, m_new + off, -NEG_INF))
            p_ref[1 - par, g] = p_f.astype(BF16)
            st_ref[g, pl.ds(ROW_M, 1), :] = m_new
            st_ref[g, pl.ds(ROW_A + 1 - par, 1), :] = jnp.exp(m_in - m_new)
            st_ref[g, pl.ds(ROW_PSUM + 1 - par, 1), :] = jnp.sum(p_f, axis=0, keepdims=True)

            a = st_ref[g, pl.ds(ROW_A + par, 1), :]
            l_new = a * st_ref[g, pl.ds(ROW_L, 1), :] + st_ref[g, pl.ds(ROW_PSUM + par, 1), :]
            st_ref[g, pl.ds(ROW_L, 1), :] = l_new
            acc_new = a * acc_ref[g] + jnp.dot(vT_ref[0, g, kc], p_ref[par, g], preferred_element_type=F32)
            acc_ref[g] = acc_new
            o_ref[0, g, qc] = (acc_new / l_new).astype(o_ref.dtype)

    def two_steps(t, carry):
        step(2 * t, 0)
        step(2 * t + 1, 1)
        return carry

    lax.fori_loop(0, (n_steps + 2) // 2, two_steps, 0)


def moba_attention(qT, k, vT, slopes, heads_per_step=4):
    b, h, nb, hd, bs = qT.shape
    hg = heads_per_step
    steps = [(i, i if t == 0 else t - 1) for i in range(nb) for t in range(i + 1)]
    qblk = jnp.asarray([s[0] for s in steps], jnp.int32)
    kblk = jnp.asarray([s[1] for s in steps], jnp.int32)
    blk = lambda s: pl.BlockSpec((1, hg) + s, lambda bi, hi, *_: (bi, hi, 0, 0, 0))
    return pl.pallas_call(
        _moba_kernel,
        out_shape=jax.ShapeDtypeStruct(qT.shape, BF16),
        grid_spec=pltpu.PrefetchScalarGridSpec(
            num_scalar_prefetch=3,
            grid=(b, h // hg),
            in_specs=[blk((nb, hd, bs)), blk((nb, bs, hd)), blk((nb, hd, bs))],
            out_specs=blk((nb, hd, bs)),
            scratch_shapes=[pltpu.VMEM((hg, nb, nb, bs), F32),
                            pltpu.VMEM((hg, 2, bs, bs), F32),
                            pltpu.VMEM((2, hg, bs, bs), F32),
                            pltpu.VMEM((2, hg, bs, bs), BF16),
                            pltpu.VMEM((hg, hd, bs), F32),
                            pltpu.VMEM((hg, 8, bs), F32)]),
        compiler_params=_cparams("parallel", "parallel"),
        name="moba_attention",
    )(slopes, qblk, kblk, qT, k, vT)


def _swa_kernel(slope_ref, sink_ref, qT_ref, k_ref, vT_ref, o_ref, pbias_ref, cbias_ref):
    nb = qT_ref.shape[2]
    w = k_ref.shape[3]
    gw = qT_ref.shape[4]
    slope = slope_ref[0]
    sink = sink_ref[0]
    key_i = lax.broadcasted_iota(jnp.int32, (w, gw), 0)
    qry_i = lax.broadcasted_iota(jnp.int32, (w, gw), 1) % w
    d_prev = (w + qry_i - key_i).astype(F32)
    d_cur = (qry_i - key_i).astype(F32)
    pbias_ref[...] = jnp.where(d_prev < w, slope * d_prev, -NEG_INF)
    cbias_ref[...] = jnp.where(d_cur >= 0, slope * d_cur, -NEG_INF)

    def q_block(n, carry):
        qT = qT_ref[0, 0, n]
        no_prev = jnp.where(n == 0, -NEG_INF, 0.0)
        zp = jnp.dot(k_ref[0, 0, n], qT, preferred_element_type=F32) - pbias_ref[...] - no_prev
        zc = jnp.dot(k_ref[0, 0, n + 1], qT, preferred_element_type=F32) - cbias_ref[...]
        m = jnp.maximum(jnp.max(zp, axis=0, keepdims=True), jnp.max(zc, axis=0, keepdims=True))
        m = jnp.maximum(m, sink)
        pp = jnp.exp(zp - m)
        pc = jnp.exp(zc - m)
        l = jnp.sum(pp, axis=0, keepdims=True) + jnp.sum(pc, axis=0, keepdims=True) + jnp.exp(sink - m)
        acc = jnp.dot(vT_ref[0, 0, n], pp.astype(BF16), preferred_element_type=F32)
        acc = acc + jnp.dot(vT_ref[0, 0, n + 1], pc.astype(BF16), preferred_element_type=F32)
        o_ref[0, 0, n] = (acc / l).astype(o_ref.dtype)
        return carry

    lax.fori_loop(0, nb, q_block, 0)


def swa_attention(qT, k, vT, slope_lanes, sink_lanes):
    b, kv, nb, hd, gw = qT.shape
    w = k.shape[3]
    idx = lambda bi, ki: (bi, ki, 0, 0, 0)
    lane = pl.BlockSpec((1, 1, gw), lambda bi, ki: (ki, 0, 0))
    return pl.pallas_call(
        _swa_kernel,
        out_shape=jax.ShapeDtypeStruct(qT.shape, BF16),
        grid=(b, kv),
        in_specs=[lane, lane,
                  pl.BlockSpec((1, 1, nb, hd, gw), idx),
                  pl.BlockSpec((1, 1, nb + 1, w, hd), idx),
                  pl.BlockSpec((1, 1, nb + 1, hd, w), idx)],
        out_specs=pl.BlockSpec((1, 1, nb, hd, gw), idx),
        scratch_shapes=[pltpu.VMEM((w, gw), F32), pltpu.VMEM((w, gw), F32)],
        compiler_params=_cparams("parallel", "parallel"),
        name="swa_attention",
    )(slope_lanes, sink_lanes, qT, k, vT)


def _swiglu(xb, wg, wu, wd):
    hg = jnp.dot(xb, wg, preferred_element_type=F32)
    hu = jnp.dot(xb, wu, preferred_element_type=F32)
    hh = hg * jax.nn.sigmoid(hg) * hu
    return jnp.dot(hh.astype(BF16), wd, preferred_element_type=F32)


def _ffn_kernel(x_ref, wg_ref, wu_ref, wd_ref, g_ref, b_ref, o_ref):
    ffn = _swiglu(x_ref[...].astype(BF16), wg_ref[...], wu_ref[...], wd_ref[...])
    y = DEEPNORM_ALPHA * x_ref[...] + ffn
    o_ref[...] = _layer_norm(y, g_ref[...], b_ref[...])


def ffn_ln(x, wg, wu, wd, g, b, tm=512):
    m, d = x.shape
    f = wg.shape[1]
    tm = min(tm, m)
    whole = lambda s: pl.BlockSpec(s, lambda t: (0,) * len(s))
    return pl.pallas_call(
        _ffn_kernel,
        out_shape=jax.ShapeDtypeStruct((m, d), F32),
        grid=(m // tm,),
        in_specs=[pl.BlockSpec((tm, d), lambda t: (t, 0)),
                  whole((d, f)), whole((d, f)), whole((f, d)), whole((1, d)), whole((1, d))],
        out_specs=pl.BlockSpec((tm, d), lambda t: (t, 0)),
        compiler_params=_cparams("parallel"),
        name="ffn_ln",
    )(x, wg, wu, wd, g.reshape(1, d), b.reshape(1, d))


def _router_kernel(x_ref, wrT_ref, gate_ref, rank_ref, count_ref):
    tt = x_ref.shape[0]
    logits = lax.dot_general(wrT_ref[...], x_ref[...], (((1,), (1,)), ((), ())),
                             preferred_element_type=F32, precision=lax.Precision.HIGHEST)
    e_i = lax.broadcasted_iota(jnp.int32, logits.shape, 0)
    work = logits
    vals, picks = [], []
    for _ in range(MOE_TOPK):
        best = jnp.max(work, axis=0, keepdims=True)
        first = jnp.min(jnp.where(work == best, e_i, N_EXPERTS), axis=0, keepdims=True)
        pick = e_i == first
        vals.append(best)
        picks.append(pick)
        work = jnp.where(pick, PICKED, work)
    ex = [jnp.exp(v - vals[0]) for v in vals]
    denom = functools.reduce(lambda a, c: a + c, ex)
    gates = jnp.zeros(logits.shape, F32)
    sel = jnp.zeros(logits.shape, F32)
    for pick, e in zip(picks, ex):
        gates = jnp.where(pick, e / denom, gates)
        sel = jnp.where(pick, 1.0, sel)
    gate_ref[...] = gates
    before = (lax.broadcasted_iota(jnp.int32, (tt, tt), 0) < lax.broadcasted_iota(jnp.int32, (tt, tt), 1))
    rank = jnp.dot(sel.astype(BF16), jnp.where(before, 1.0, 0.0).astype(BF16), preferred_element_type=F32)
    rank_ref[...] = jnp.where(sel > 0.0, rank, -1.0)
    count_ref[0] = jnp.broadcast_to(jnp.sum(sel, axis=1, keepdims=True), count_ref.shape[1:])


def router(x, wrT, tt):
    n, d = x.shape
    ne = wrT.shape[0]
    return pl.pallas_call(
        _router_kernel,
        out_shape=(jax.ShapeDtypeStruct((ne, n), F32), jax.ShapeDtypeStruct((ne, n), F32),
                   jax.ShapeDtypeStruct((n // tt, ne, 128), F32)),
        grid=(n // tt,),
        in_specs=[pl.BlockSpec((tt, d), lambda t: (t, 0)), pl.BlockSpec((ne, d), lambda t: (0, 0))],
        out_specs=(pl.BlockSpec((ne, tt), lambda t: (0, t)), pl.BlockSpec((ne, tt), lambda t: (0, t)),
                   pl.BlockSpec((1, ne, 128), lambda t: (t, 0, 0))),
        compiler_params=_cparams("parallel"),
        name="router",
    )(x, wrT)


MOE_ROWS = 128
MOE_GROUP = 4
MOE_TOKEN_TILE = 1024
MOE_FF_SPLIT = 2


def _moe_kernel(count_ref, x_ref, gate_ref, rank_ref, wg_ref, wu_ref, wd_ref, g_ref, b_ref, o_ref,
                xb_ref, xe_ref, ye_ref):
    t, e, half = pl.program_id(0), pl.program_id(1), pl.program_id(2)
    ne, nh = pl.num_programs(1), pl.num_programs(2)
    tt = x_ref.shape[0]
    rows = MOE_ROWS
    n_chunks = (count_ref[t * ne + e] + rows - 1) // rows
    rank_row = rank_ref[pl.ds(e, 1), :]

    def one_hot(c, k):
        row_i = lax.broadcasted_iota(jnp.int32, (k * rows, tt), 0).astype(F32)
        return rank_row == row_i + lax.convert_element_type(c * rows, F32)

    def row_groups(fn):
        big = MOE_GROUP
        n_big = n_chunks // big

        def body(i, carry):
            fn(i * big, big)
            return carry
        lax.fori_loop(0, n_big, body, 0)
        for k in range(1, big):
            @pl.when(n_chunks - n_big * big == k)
            def _():
                fn(n_big * big, k)

    @pl.when(jnp.logical_and(e == 0, half == 0))
    def _():
        o_ref[...] = jnp.zeros_like(o_ref)
        xb_ref[...] = x_ref[...].astype(BF16)

    @pl.when(half == 0)
    def _():
        def gather(c, k):
            hot = jnp.where(one_hot(c, k), 1.0, 0.0).astype(BF16)
            r0 = pl.multiple_of(c * rows, rows)
            xe_ref[pl.ds(r0, k * rows), :] = jnp.dot(hot, xb_ref[...], preferred_element_type=F32).astype(BF16)
            ye_ref[pl.ds(r0, k * rows), :] = jnp.zeros((k * rows, ye_ref.shape[1]), F32)
        row_groups(gather)

    def expert(c, k):
        r0 = pl.multiple_of(c * rows, rows)
        ye = _swiglu(xe_ref[pl.ds(r0, k * rows), :], wg_ref[0, 0], wu_ref[0, 0], wd_ref[0, 0])
        ye_ref[pl.ds(r0, k * rows), :] += ye
    row_groups(expert)

    @pl.when(half == nh - 1)
    def _():
        gate_row = gate_ref[pl.ds(e, 1), :]

        def scatter(c, k):
            hot = one_hot(c, k)
            r0 = pl.multiple_of(c * rows, rows)
            w_col = jnp.sum(jnp.where(hot, gate_row, 0.0), axis=1, keepdims=True)
            yw = (ye_ref[pl.ds(r0, k * rows), :] * w_col).astype(BF16)
            o_ref[...] += lax.dot_general(jnp.where(hot, 1.0, 0.0).astype(BF16), yw,
                                          (((0,), (0,)), ((), ())), preferred_element_type=F32)
        row_groups(scatter)

    @pl.when(jnp.logical_and(e == ne - 1, half == nh - 1))
    def _():
        y = DEEPNORM_ALPHA * x_ref[...] + o_ref[...]
        o_ref[...] = _layer_norm(y, g_ref[...], b_ref[...])


def moe_ffn_ln(x, gate, rank, count, wg, wu, wd, g, b, tt):
    n, d = x.shape
    ne, fh = wg.shape[0], wg.shape[3]
    wspec = lambda s: pl.BlockSpec((1, 1) + s, lambda t, e, h, *_: (e, h, 0, 0))
    row = pl.BlockSpec((1, d), lambda t, e, h, *_: (0, 0))
    return pl.pallas_call(
        _moe_kernel,
        out_shape=jax.ShapeDtypeStruct((n, d), F32),
        grid_spec=pltpu.PrefetchScalarGridSpec(
            num_scalar_prefetch=1,
            grid=(n // tt, ne, wg.shape[1]),
            in_specs=[pl.BlockSpec((tt, d), lambda t, e, h, *_: (t, 0)),
                      pl.BlockSpec((ne, tt), lambda t, e, h, *_: (0, t)),
                      pl.BlockSpec((ne, tt), lambda t, e, h, *_: (0, t)),
                      wspec((d, fh)), wspec((d, fh)), wspec((fh, d)), row, row],
            out_specs=pl.BlockSpec((tt, d), lambda t, e, h, *_: (t, 0)),
            scratch_shapes=[pltpu.VMEM((tt, d), BF16), pltpu.VMEM((tt, d), BF16), pltpu.VMEM((tt, d), F32)]),
        compiler_params=_cparams("parallel", "arbitrary", "arbitrary"),
        name="moe_ffn_ln",
    )(count, x, gate, rank, wg, wu, wd, g.reshape(1, d), b.reshape(1, d))


def _alibi_slopes(n):
    return jnp.exp2(-8.0 * jnp.arange(1, n + 1, dtype=F32) / n)


def _split_cols(w, parts):
    e, d, f = w.shape
    return w.reshape(e, d, parts, f // parts).transpose(0, 2, 1, 3).astype(BF16)


def _split_rows(w, parts):
    e, f, d = w.shape
    return w.reshape(e, parts, f // parts, d).astype(BF16)


def kernel(x, w_qkv_a, w_o_a, w_kv_shared, w_q_b, w_o_b, sinks_b, w_gate_d, w_up_d, w_down_d,
           w_router, w_gate_e, w_up_e, w_down_e, ln_gain, ln_bias):
    B, T, D = x.shape
    hd = HEAD_DIM
    H = D // hd
    N = B * T
    xs = x.reshape(N, D)
    q_scale = hd ** -0.5

    HA = w_qkv_a.shape[-1] // 3
    col_scale = jnp.concatenate([jnp.full((HA,), q_scale, F32), jnp.ones((2 * HA,), F32)])
    qkv = matmul(xs, (w_qkv_a[0] * col_scale).astype(BF16), BF16)
    bs = MOBA_BLOCK
    nb = T // bs
    q, k, v = (qkv[:, i * HA:(i + 1) * HA].reshape(B, nb, bs, H, hd) for i in range(3))
    qT = q.transpose(0, 3, 1, 4, 2)
    kb = k.transpose(0, 3, 1, 2, 4)
    vT = v.transpose(0, 3, 1, 4, 2)
    oT = moba_attention(qT, kb, vT, _alibi_slopes(H))
    o = oT.transpose(0, 2, 4, 1, 3).reshape(N, HA)
    xs = proj_ln(o, w_o_a[0].astype(BF16), xs, ln_gain[0, 0], ln_bias[0, 0])

    xs = ffn_ln(xs, w_gate_d[0].astype(BF16), w_up_d[0].astype(BF16), w_down_d[0].astype(BF16),
                ln_gain[0, 1], ln_bias[0, 1])

    KV = N_KV_B
    G = H // KV
    W = WINDOW
    nw = T // W
    w_qkv_b = jnp.concatenate([w_q_b[0] * q_scale, w_kv_shared], axis=1).astype(BF16)
    qkv = matmul(xs, w_qkv_b, BF16, tn=w_qkv_b.shape[1])
    q = qkv[:, :D].reshape(B, nw, W, KV, G, hd)
    qT = q.transpose(0, 3, 1, 5, 4, 2).reshape(B, KV, nw, hd, G * W)
    k = qkv[:, D:D + KV * hd].reshape(B, nw, W, KV, hd).transpose(0, 3, 1, 2, 4)
    v = qkv[:, D + KV * hd:].reshape(B, nw, W, KV, hd).transpose(0, 3, 1, 4, 2)
    kb = jnp.pad(k, ((0, 0), (0, 0), (1, 0), (0, 0), (0, 0)))
    vT = jnp.pad(v, ((0, 0), (0, 0), (1, 0), (0, 0), (0, 0)))
    slope_lanes = jnp.repeat(_alibi_slopes(H).reshape(KV, G), W, axis=1).reshape(KV, 1, G * W)
    sink_lanes = jnp.repeat(sinks_b[0].astype(F32).reshape(KV, G), W, axis=1).reshape(KV, 1, G * W)
    oT = swa_attention(qT, kb, vT, slope_lanes, sink_lanes)
    o = oT.reshape(B, KV, nw, hd, G, W).transpose(0, 2, 5, 1, 4, 3).reshape(N, D)
    xs = proj_ln(o, w_o_b[0].astype(BF16), xs, ln_gain[1, 0], ln_bias[1, 0])

    tt = min(MOE_TOKEN_TILE, N)
    gate, rank, count = router(xs, w_router[0].T, tt)
    count = count[:, :, 0].astype(jnp.int32).reshape(-1)
    xs = moe_ffn_ln(xs, gate, rank, count, _split_cols(w_gate_e[0], MOE_FF_SPLIT),
                    _split_cols(w_up_e[0], MOE_FF_SPLIT), _split_rows(w_down_e[0], MOE_FF_SPLIT),
                    ln_gain[1, 1], ln_bias[1, 1], tt)
    return xs.reshape(B, T, D)
```

```python
import functools

import jax
import jax.numpy as jnp
from jax import lax
from jax.experimental import pallas as pl
from jax.experimental.pallas import tpu as pltpu

HEAD_DIM = 64
MOBA_BLOCK = 256
MOBA_TOPK = 3
N_KV_B = 2
WINDOW = 128
N_EXPERTS = 8
MOE_TOPK = 2
DEPTH = 2
DEEPNORM_ALPHA = (2 * DEPTH) ** 0.25
LN_EPS = 1e-5
NEG_INF = -1e30
PICKED = -3e38
LOG2E = 1.4426950408889634
VMEM_LIMIT = 56 * 1024 * 1024

F32 = jnp.float32
BF16 = jnp.bfloat16


def _cparams(*sem):
    return pltpu.CompilerParams(dimension_semantics=sem, vmem_limit_bytes=VMEM_LIMIT)


def _layer_norm(y, g, b):
    mu = jnp.mean(y, axis=-1, keepdims=True)
    yc = y - mu
    var = jnp.mean(yc * yc, axis=-1, keepdims=True)
    return yc * lax.rsqrt(var + LN_EPS) * g + b


_NT = (((1,), (1,)), ((), ()))
_TN = (((0,), (0,)), ((), ()))


def _qkv_kernel(x_ref, wqT_ref, wk_ref, wvT_ref, qT_ref, k_ref, vT_ref, *, q_groups):
    xb = x_ref[...].astype(BF16)
    k_ref[...] = jnp.dot(xb, wk_ref[...], preferred_element_type=F32).astype(BF16)
    qT = lax.dot_general(wqT_ref[...], xb, _NT, preferred_element_type=F32).astype(BF16)
    vT = lax.dot_general(wvT_ref[...], xb, _NT, preferred_element_type=F32).astype(BF16)
    _, hv, nblk, hd, bs = vT_ref.shape
    _, hq, _, padded, ql = qT_ref.shape
    pad = padded // hd
    zero = jnp.zeros((hd, bs), BF16)
    for blk in range(nblk):
        cols = slice(blk * bs, (blk + 1) * bs)
        for h in range(hv):
            vT_ref[0, h, blk] = vT[h * hd:(h + 1) * hd, cols]
        for h in range(hq):
            lanes = []
            for g in range(q_groups):
                r0 = (h * q_groups + g) * hd
                parts = [zero] * pad
                parts[h % pad] = qT[r0:r0 + hd, cols]
                lanes.append(jnp.concatenate(parts, axis=0))
            qT_ref[0, h, blk] = lanes[0] if q_groups == 1 else jnp.concatenate(lanes, axis=1)


def qkv_proj(x, wqT, wk, wvT, batch, hq, hv, hd, bs, pad, q_groups, tm=1024):
    n, d = x.shape
    t = n // batch
    tm = min(tm, t)
    nblk = tm // bs
    steps = t // tm
    whole = lambda a: pl.BlockSpec(a.shape, lambda b, i: (0, 0))
    return pl.pallas_call(
        functools.partial(_qkv_kernel, q_groups=q_groups),
        out_shape=(jax.ShapeDtypeStruct((batch, hq, t // bs, pad * hd, q_groups * bs), BF16),
                   jax.ShapeDtypeStruct((n, wk.shape[1]), BF16),
                   jax.ShapeDtypeStruct((batch, hv, t // bs, hd, bs), BF16)),
        grid=(batch, steps),
        in_specs=[pl.BlockSpec((tm, d), lambda b, i: (b * steps + i, 0)), whole(wqT), whole(wk), whole(wvT)],
        out_specs=(pl.BlockSpec((1, hq, nblk, pad * hd, q_groups * bs), lambda b, i: (b, 0, i, 0, 0)),
                   pl.BlockSpec((tm, wk.shape[1]), lambda b, i: (b * steps + i, 0)),
                   pl.BlockSpec((1, hv, nblk, hd, bs), lambda b, i: (b, 0, i, 0, 0))),
        compiler_params=_cparams("parallel", "parallel"),
        name="qkv_proj",
    )(x, wqT, wk, wvT)


def _oT_proj_ln_kernel(oT_ref, w_ref, x_ref, g_ref, b_ref, o_ref, *, bs):
    _, ho, nblk, hd, lanes = oT_ref.shape
    tiles = []
    for blk in range(nblk):
        heads = [oT_ref[0, h, blk, :, g * bs:(g + 1) * bs] for h in range(ho) for g in range(lanes // bs)]
        tiles.append(jnp.concatenate(heads, axis=0))
    aT = tiles[0] if nblk == 1 else jnp.concatenate(tiles, axis=1)
    mix = lax.dot_general(aT, w_ref[...], _TN, preferred_element_type=F32)
    y = DEEPNORM_ALPHA * x_ref[...] + mix
    o_ref[...] = _layer_norm(y, g_ref[...], b_ref[...])


def oT_proj_ln(oT, w, x, g, b, bs, tm=512):
    batch, ho, nb, hd, lanes = oT.shape
    n, d = x.shape
    t = n // batch
    tm = min(tm, t)
    nblk = tm // bs
    steps = t // tm
    row = pl.BlockSpec((1, d), lambda bi, i: (0, 0))
    return pl.pallas_call(
        functools.partial(_oT_proj_ln_kernel, bs=bs),
        out_shape=jax.ShapeDtypeStruct((n, d), F32),
        grid=(batch, steps),
        in_specs=[pl.BlockSpec((1, ho, nblk, hd, lanes), lambda bi, i: (bi, 0, i, 0, 0)),
                  pl.BlockSpec(w.shape, lambda bi, i: (0, 0)),
                  pl.BlockSpec((tm, d), lambda bi, i: (bi * steps + i, 0)), row, row],
        out_specs=pl.BlockSpec((tm, d), lambda bi, i: (bi * steps + i, 0)),
        compiler_params=_cparams("parallel", "parallel"),
        name="proj_ln",
    )(oT, w, x, g.reshape(1, d), b.reshape(1, d))


def _moba_kernel(slope_ref, qblk_ref, kblk_ref, qT_ref, k_ref, vT_ref, o_ref,
                 sel_ref, bias_ref, z_ref, p_ref, acc_ref, st_ref):
    _, hg, nb, hd, bs = vT_ref.shape
    pair = qT_ref.shape[3]
    n_steps = nb * (nb + 1) // 2
    heads = range(hg)
    slopes = [slope_ref[pl.program_id(1) * hg + g] * LOG2E for g in heads]

    key_i = lax.broadcasted_iota(jnp.int32, (bs, bs), 0)
    qry_i = lax.broadcasted_iota(jnp.int32, (bs, bs), 1)
    rel = (qry_i - key_i).astype(F32)
    blk_i = lax.broadcasted_iota(jnp.int32, (nb, bs), 0)
    klanes = [pl.ds((g * hd) // pair * pair, pair) for g in heads]
    kmean = jnp.sum(k_ref[...].astype(F32).reshape(nb, bs, hg * hd), axis=1) * (1.0 / bs)
    km_hi = kmean.astype(BF16)
    km_lo = (kmean - km_hi.astype(F32)).astype(BF16)
    km2 = []
    for g in heads:
        bias = slopes[g] * rel
        bias_ref[g, 0] = bias
        bias_ref[g, 1] = jnp.where(rel >= 0, bias, -NEG_INF)
        lo = (g * hd) // pair * pair
        km2.append(jnp.concatenate([km_hi[:, lo:lo + pair], km_lo[:, lo:lo + pair]], axis=0))

    def select(i, carry):
        past = blk_i < i
        own = jnp.where(blk_i == i, 1.0, 0.0)
        for g in heads:
            g2 = jnp.dot(km2[g], qT_ref[0, g, i], preferred_element_type=F32)
            gate = jnp.where(past, g2[:nb] + g2[nb:], NEG_INF)
            sel = jnp.zeros((nb, bs), F32)
            for _ in range(MOBA_TOPK):
                best = jnp.max(gate, axis=0, keepdims=True)
                first = jnp.min(jnp.where(gate == best, blk_i, nb), axis=0, keepdims=True)
                pick = blk_i == first
                sel = jnp.where(pick, 1.0, sel)
                gate = jnp.where(pick, PICKED, gate)
            sel_ref[g, i] = jnp.where(past, sel, own)
        return carry

    lax.fori_loop(0, nb, select, 0)

    ROW_M, ROW_A = 0, 1
    ones_rows = jnp.ones((acc_ref.shape[1] - hd, bs), BF16)
    z_ref[...] = jnp.zeros_like(z_ref)
    p_ref[...] = jnp.zeros_like(p_ref)
    acc0 = jnp.where(lax.broadcasted_iota(jnp.int32, acc_ref.shape[1:], 0) < hd, 0.0, 1.0)
    st0 = jnp.where(lax.broadcasted_iota(jnp.int32, (8, bs), 0) == ROW_M, NEG_INF, 1.0)
    for g in heads:
        acc_ref[g] = acc0
        st_ref[g] = st0

    def step(s, par):
        sa = jnp.minimum(s, n_steps - 1)
        sb = jnp.clip(s - 1, 0, n_steps - 1)
        sc = jnp.clip(s - 2, 0, n_steps - 1)
        qa, ka = qblk_ref[sa], kblk_ref[sa]
        qb, kb = qblk_ref[sb], kblk_ref[sb]
        qc, kc = qblk_ref[sc], kblk_ref[sc]
        own_a = (qa == ka).astype(jnp.int32)
        own = qb == kb
        for g in heads:
            k_blk = k_ref[pl.ds(pl.multiple_of(ka * bs, bs), bs), klanes[g]]
            scores = jnp.dot(k_blk, qT_ref[0, g, qa], preferred_element_type=F32)
            z_ref[par, g] = scores - bias_ref[g, own_a]

            off = slopes[g] * lax.convert_element_type((qb - kb) * bs, F32)
            chosen = sel_ref[g, qb, pl.ds(kb, 1), :] > 0.0
            m_in = jnp.where(own, NEG_INF, st_ref[g, pl.ds(ROW_M, 1), :])
            tile_max = jnp.max(z_ref[1 - par, g], axis=0, keepdims=True)
            m_new = jnp.where(chosen, jnp.maximum(m_in, tile_max - off), m_in)
            p_f = jnp.exp2(z_ref[1 - par, g] - jnp.where(chosen, m_new + off, -NEG_INF))
            p_ref[1 - par, g] = p_f.astype(BF16)
            st_ref[g, pl.ds(ROW_M, 1), :] = m_new
            st_ref[g, pl.ds(ROW_A + 1 - par, 1), :] = jnp.exp2(m_in - m_new)

            a = st_ref[g, pl.ds(ROW_A + par, 1), :]
            v_aug = jnp.concatenate([vT_ref[0, g, kc], ones_rows], axis=0)
            acc_new = a * acc_ref[g] + jnp.dot(v_aug, p_ref[par, g], preferred_element_type=F32)
            acc_ref[g] = acc_new
            o_ref[0, g, qc] = (acc_new[:hd] / acc_new[hd:hd + 1]).astype(o_ref.dtype)

    def two_steps(t, carry):
        step(2 * t, 0)
        step(2 * t + 1, 1)
        return carry

    lax.fori_loop(0, (n_steps + 2) // 2, two_steps, 0)


def moba_attention(qT, k, vT, slopes, heads_per_step=4):
    b, h, nb, hd, bs = vT.shape
    hg = heads_per_step
    assert (nb * (nb + 1) // 2) % 2 == 0, "the pipelined loop is unrolled by two"
    steps = [(i, i if t == 0 else t - 1) for i in range(nb) for t in range(i + 1)]
    qblk = jnp.asarray([s[0] for s in steps], jnp.int32)
    kblk = jnp.asarray([s[1] for s in steps], jnp.int32)
    blk = lambda s: pl.BlockSpec((1, hg) + s, lambda bi, hi, *_: (bi, hi, 0, 0, 0))
    return pl.pallas_call(
        _moba_kernel,
        out_shape=jax.ShapeDtypeStruct(vT.shape, BF16),
        grid_spec=pltpu.PrefetchScalarGridSpec(
            num_scalar_prefetch=3,
            grid=(b, h // hg),
            in_specs=[blk((nb, qT.shape[3], bs)),
                      pl.BlockSpec((nb * bs, hg * hd), lambda bi, hi, *_: (bi, hi)),
                      blk((nb, hd, bs))],
            out_specs=blk((nb, hd, bs)),
            scratch_shapes=[pltpu.VMEM((hg, nb, nb, bs), F32),
                            pltpu.VMEM((hg, 2, bs, bs), F32),
                            pltpu.VMEM((2, hg, bs, bs), F32),
                            pltpu.VMEM((2, hg, bs, bs), BF16),
                            pltpu.VMEM((hg, hd + 16, bs), F32),
                            pltpu.VMEM((hg, 8, bs), F32)]),
        compiler_params=_cparams("parallel", "parallel"),
        name="moba_attention",
    )(slopes, qblk, kblk, qT, k, vT)


def _swa_kernel(slope_ref, sink_ref, qT_ref, k_ref, vT_ref, o_ref, pbias_ref, cbias_ref):
    nb = qT_ref.shape[2]
    w = vT_ref.shape[4]
    gw = qT_ref.shape[4]
    slope = slope_ref[0]
    sink = sink_ref[0]
    key_i = lax.broadcasted_iota(jnp.int32, (w, gw), 0)
    qry_i = lax.broadcasted_iota(jnp.int32, (w, gw), 1) % w
    d_prev = (w + qry_i - key_i).astype(F32)
    d_cur = (qry_i - key_i).astype(F32)
    pbias_ref[...] = jnp.where(d_prev < w, slope * d_prev, -NEG_INF)
    cbias_ref[...] = jnp.where(d_cur >= 0, slope * d_cur, -NEG_INF)

    def q_block(n, carry):
        qT = qT_ref[0, 0, n]
        prev = jnp.maximum(n - 1, 0)
        no_prev = jnp.where(n == 0, -NEG_INF, 0.0)
        k_prev = k_ref[pl.ds(pl.multiple_of(prev * w, w), w), :]
        k_cur = k_ref[pl.ds(pl.multiple_of(n * w, w), w), :]
        zp = jnp.dot(k_prev, qT, preferred_element_type=F32) - pbias_ref[...] - no_prev
        zc = jnp.dot(k_cur, qT, preferred_element_type=F32) - cbias_ref[...]
        m = jnp.maximum(jnp.max(zp, axis=0, keepdims=True), jnp.max(zc, axis=0, keepdims=True))
        m = jnp.maximum(m, sink)
        pp = jnp.exp(zp - m)
        pc = jnp.exp(zc - m)
        l = jnp.sum(pp, axis=0, keepdims=True) + jnp.sum(pc, axis=0, keepdims=True) + jnp.exp(sink - m)
        acc = jnp.dot(vT_ref[0, 0, prev], pp.astype(BF16), preferred_element_type=F32)
        acc = acc + jnp.dot(vT_ref[0, 0, n], pc.astype(BF16), preferred_element_type=F32)
        o_ref[0, 0, n] = (acc / l).astype(o_ref.dtype)
        return carry

    lax.fori_loop(0, nb, q_block, 0)


def swa_attention(qT, k, vT, slope_lanes, sink_lanes):
    b, kv, nb, hd, w = vT.shape
    gw = qT.shape[4]
    idx = lambda bi, ki: (bi, ki, 0, 0, 0)
    lane = pl.BlockSpec((1, 1, gw), lambda bi, ki: (ki, 0, 0))
    return pl.pallas_call(
        _swa_kernel,
        out_shape=jax.ShapeDtypeStruct((b, kv, nb, hd, gw), BF16),
        grid=(b, kv),
        in_specs=[lane, lane,
                  pl.BlockSpec((1, 1, nb, qT.shape[3], gw), idx),
                  pl.BlockSpec((nb * w, k.shape[1]), lambda bi, ki: (bi, 0)),
                  pl.BlockSpec((1, 1, nb, hd, w), idx)],
        out_specs=pl.BlockSpec((1, 1, nb, hd, gw), idx),
        scratch_shapes=[pltpu.VMEM((w, gw), F32), pltpu.VMEM((w, gw), F32)],
        compiler_params=_cparams("parallel", "parallel"),
        name="swa_attention",
    )(slope_lanes, sink_lanes, qT, k, vT)


def _swiglu(xb, wg, wu, wd):
    hg = jnp.dot(xb, wg, preferred_element_type=F32)
    hu = jnp.dot(xb, wu, preferred_element_type=F32)
    hh = hg * jax.nn.sigmoid(hg) * hu
    return jnp.dot(hh.astype(BF16), wd, preferred_element_type=F32)


def _ffn_kernel(x_ref, wg_ref, wu_ref, wd_ref, g_ref, b_ref, o_ref):
    ffn = _swiglu(x_ref[...].astype(BF16), wg_ref[...], wu_ref[...], wd_ref[...])
    y = DEEPNORM_ALPHA * x_ref[...] + ffn
    o_ref[...] = _layer_norm(y, g_ref[...], b_ref[...])


def ffn_ln(x, wg, wu, wd, g, b, tm=512):
    m, d = x.shape
    f = wg.shape[1]
    tm = min(tm, m)
    whole = lambda s: pl.BlockSpec(s, lambda t: (0,) * len(s))
    return pl.pallas_call(
        _ffn_kernel,
        out_shape=jax.ShapeDtypeStruct((m, d), F32),
        grid=(m // tm,),
        in_specs=[pl.BlockSpec((tm, d), lambda t: (t, 0)),
                  whole((d, f)), whole((d, f)), whole((f, d)), whole((1, d)), whole((1, d))],
        out_specs=pl.BlockSpec((tm, d), lambda t: (t, 0)),
        compiler_params=_cparams("parallel"),
        name="ffn_ln",
    )(x, wg, wu, wd, g.reshape(1, d), b.reshape(1, d))


def _router_kernel(x_ref, wrT_ref, gate_ref, rank_ref, count_ref):
    tt = x_ref.shape[0]
    logits = lax.dot_general(wrT_ref[...], x_ref[...], (((1,), (1,)), ((), ())),
                             preferred_element_type=F32, precision=lax.Precision.HIGHEST)
    e_i = lax.broadcasted_iota(jnp.int32, logits.shape, 0)
    work = logits
    vals, picks = [], []
    for _ in range(MOE_TOPK):
        best = jnp.max(work, axis=0, keepdims=True)
        first = jnp.min(jnp.where(work == best, e_i, N_EXPERTS), axis=0, keepdims=True)
        pick = e_i == first
        vals.append(best)
        picks.append(pick)
        work = jnp.where(pick, PICKED, work)
    ex = [jnp.exp(v - vals[0]) for v in vals]
    denom = functools.reduce(lambda a, c: a + c, ex)
    gates = jnp.zeros(logits.shape, F32)
    sel = jnp.zeros(logits.shape, F32)
    for pick, e in zip(picks, ex):
        gates = jnp.where(pick, e / denom, gates)
        sel = jnp.where(pick, 1.0, sel)
    gate_ref[...] = gates
    before = (lax.broadcasted_iota(jnp.int32, (tt, tt), 0) < lax.broadcasted_iota(jnp.int32, (tt, tt), 1))
    rank = jnp.dot(sel.astype(BF16), jnp.where(before, 1.0, 0.0).astype(BF16), preferred_element_type=F32)
    rank_ref[...] = jnp.where(sel > 0.0, rank, -1.0)
    count_ref[0] = jnp.broadcast_to(jnp.sum(sel, axis=1, keepdims=True), count_ref.shape[1:])


def router(x, wrT, tt):
    n, d = x.shape
    ne = wrT.shape[0]
    return pl.pallas_call(
        _router_kernel,
        out_shape=(jax.ShapeDtypeStruct((ne, n), F32), jax.ShapeDtypeStruct((ne, n), F32),
                   jax.ShapeDtypeStruct((n // tt, ne, 128), F32)),
        grid=(n // tt,),
        in_specs=[pl.BlockSpec((tt, d), lambda t: (t, 0)), pl.BlockSpec((ne, d), lambda t: (0, 0))],
        out_specs=(pl.BlockSpec((ne, tt), lambda t: (0, t)), pl.BlockSpec((ne, tt), lambda t: (0, t)),
                   pl.BlockSpec((1, ne, 128), lambda t: (t, 0, 0))),
        compiler_params=_cparams("parallel"),
        name="router",
    )(x, wrT)


MOE_ROWS = 128
MOE_GROUP = 4
MOE_TOKEN_TILE = 1024
MOE_FF_SPLIT = 2


def _moe_kernel(count_ref, x_ref, gate_ref, rank_ref, wg_ref, wu_ref, wd_ref, g_ref, b_ref, o_ref,
                xb_ref, xe_ref, ye_ref):
    t, e, half = pl.program_id(0), pl.program_id(1), pl.program_id(2)
    ne, nh = pl.num_programs(1), pl.num_programs(2)
    tt = x_ref.shape[0]
    rows = MOE_ROWS
    n_chunks = (count_ref[t * ne + e] + rows - 1) // rows
    rank_row = rank_ref[pl.ds(e, 1), :]

    def one_hot(c, k):
        row_i = lax.broadcasted_iota(jnp.int32, (k * rows, tt), 0).astype(F32)
        return rank_row == row_i + lax.convert_element_type(c * rows, F32)

    def row_groups(fn):
        big = MOE_GROUP
        n_big = n_chunks // big

        def body(i, carry):
            fn(i * big, big)
            return carry
        lax.fori_loop(0, n_big, body, 0)
        for k in range(1, big):
            @pl.when(n_chunks - n_big * big == k)
            def _():
                fn(n_big * big, k)

    @pl.when(jnp.logical_and(e == 0, half == 0))
    def _():
        o_ref[...] = jnp.zeros_like(o_ref)
        xb_ref[...] = x_ref[...].astype(BF16)

    @pl.when(half == 0)
    def _():
        def gather(c, k):
            hot = jnp.where(one_hot(c, k), 1.0, 0.0).astype(BF16)
            r0 = pl.multiple_of(c * rows, rows)
            xe_ref[pl.ds(r0, k * rows), :] = jnp.dot(hot, xb_ref[...], preferred_element_type=F32).astype(BF16)
            ye_ref[pl.ds(r0, k * rows), :] = jnp.zeros((k * rows, ye_ref.shape[1]), F32)
        row_groups(gather)

    def expert(c, k):
        r0 = pl.multiple_of(c * rows, rows)
        ye = _swiglu(xe_ref[pl.ds(r0, k * rows), :], wg_ref[0], wu_ref[0], wd_ref[0])
        ye_ref[pl.ds(r0, k * rows), :] += ye
    row_groups(expert)

    @pl.when(half == nh - 1)
    def _():
        gate_row = gate_ref[pl.ds(e, 1), :]

        def scatter(c, k):
            hot = one_hot(c, k)
            r0 = pl.multiple_of(c * rows, rows)
            w_col = jnp.sum(jnp.where(hot, gate_row, 0.0), axis=1, keepdims=True)
            yw = (ye_ref[pl.ds(r0, k * rows), :] * w_col).astype(BF16)
            o_ref[...] += lax.dot_general(jnp.where(hot, 1.0, 0.0).astype(BF16), yw,
                                          (((0,), (0,)), ((), ())), preferred_element_type=F32)
        row_groups(scatter)

    @pl.when(jnp.logical_and(e == ne - 1, half == nh - 1))
    def _():
        y = DEEPNORM_ALPHA * x_ref[...] + o_ref[...]
        o_ref[...] = _layer_norm(y, g_ref[...], b_ref[...])


def moe_ffn_ln(x, gate, rank, count, wg, wu, wd, g, b, tt):
    n, d = x.shape
    ne, fh = wg.shape[0], wg.shape[2] // MOE_FF_SPLIT
    w_cols = pl.BlockSpec((1, d, fh), lambda t, e, h, *_: (e, 0, h))
    w_rows = pl.BlockSpec((1, fh, d), lambda t, e, h, *_: (e, h, 0))
    row = pl.BlockSpec((1, d), lambda t, e, h, *_: (0, 0))
    return pl.pallas_call(
        _moe_kernel,
        out_shape=jax.ShapeDtypeStruct((n, d), F32),
        grid_spec=pltpu.PrefetchScalarGridSpec(
            num_scalar_prefetch=1,
            grid=(n // tt, ne, MOE_FF_SPLIT),
            in_specs=[pl.BlockSpec((tt, d), lambda t, e, h, *_: (t, 0)),
                      pl.BlockSpec((ne, tt), lambda t, e, h, *_: (0, t)),
                      pl.BlockSpec((ne, tt), lambda t, e, h, *_: (0, t)),
                      w_cols, w_cols, w_rows, row, row],
            out_specs=pl.BlockSpec((tt, d), lambda t, e, h, *_: (t, 0)),
            scratch_shapes=[pltpu.VMEM((tt, d), BF16), pltpu.VMEM((tt, d), BF16), pltpu.VMEM((tt, d), F32)]),
        compiler_params=_cparams("parallel", "arbitrary", "arbitrary"),
        name="moe_ffn_ln",
    )(count, x, gate, rank, wg, wu, wd, g.reshape(1, d), b.reshape(1, d))


def _alibi_slopes(n):
    return jnp.exp2(-8.0 * jnp.arange(1, n + 1, dtype=F32) / n)


def kernel(x, w_qkv_a, w_o_a, w_kv_shared, w_q_b, w_o_b, sinks_b, w_gate_d, w_up_d, w_down_d,
           w_router, w_gate_e, w_up_e, w_down_e, ln_gain, ln_bias):
    B, T, D = x.shape
    hd = HEAD_DIM
    H = D // hd
    N = B * T
    xs = x.reshape(N, D)
    q_scale = hd ** -0.5

    HA = w_qkv_a.shape[-1] // 3
    wq, wk, wv = (w_qkv_a[0][:, i * HA:(i + 1) * HA] for i in range(3))
    qT, k, vT = qkv_proj(xs, (wq * (q_scale * LOG2E)).T.astype(BF16), wk.astype(BF16), wv.T.astype(BF16),
                         batch=B, hq=H, hv=H, hd=hd, bs=MOBA_BLOCK, pad=2, q_groups=1)
    oT = moba_attention(qT, k, vT, _alibi_slopes(H))
    xs = oT_proj_ln(oT, w_o_a[0].astype(BF16), xs, ln_gain[0, 0], ln_bias[0, 0], MOBA_BLOCK)

    xs = ffn_ln(xs, w_gate_d[0].astype(BF16), w_up_d[0].astype(BF16), w_down_d[0].astype(BF16),
                ln_gain[0, 1], ln_bias[0, 1])

    KV = N_KV_B
    G = H // KV
    W = WINDOW
    wk, wv = w_kv_shared[:, :KV * hd], w_kv_shared[:, KV * hd:]
    qT, k, vT = qkv_proj(xs, (w_q_b[0] * q_scale).T.astype(BF16), wk.astype(BF16), wv.T.astype(BF16),
                         batch=B, hq=KV, hv=KV, hd=hd, bs=W, pad=KV, q_groups=G)
    slope_lanes = jnp.repeat(_alibi_slopes(H).reshape(KV, G), W, axis=1).reshape(KV, 1, G * W)
    sink_lanes = jnp.repeat(sinks_b[0].astype(F32).reshape(KV, G), W, axis=1).reshape(KV, 1, G * W)
    oT = swa_attention(qT, k, vT, slope_lanes, sink_lanes)
    xs = oT_proj_ln(oT, w_o_b[0].astype(BF16), xs, ln_gain[1, 0], ln_bias[1, 0], W)

    tt = min(MOE_TOKEN_TILE, N)
    gate, rank, count = router(xs, w_router[0].T, tt)
    count = count[:, :, 0].astype(jnp.int32).reshape(-1)
    xs = moe_ffn_ln(xs, gate, rank, count, w_gate_e[0].astype(BF16), w_up_e[0].astype(BF16),
                    w_down_e[0].astype(BF16), ln_gain[1, 1], ln_bias[1, 1], tt)
    return xs.reshape(B, T, D)
```

```python
import functools

import jax
import jax.numpy as jnp
from jax import lax
from jax.experimental import pallas as pl
from jax.experimental.pallas import tpu as pltpu

HEAD_DIM = 64
MOBA_BLOCK = 256
MOBA_TOPK = 3
MOBA_UNROLL = 4
N_KV_B = 2
WINDOW = 128
N_EXPERTS = 8
MOE_TOPK = 2
DEPTH = 2
DEEPNORM_ALPHA = (2 * DEPTH) ** 0.25
LN_EPS = 1e-5
NEG_INF = -1e30
PICKED = -3e38
LOG2E = 1.4426950408889634
VMEM_LIMIT = 56 * 1024 * 1024

F32 = jnp.float32
BF16 = jnp.bfloat16


def _cparams(*sem):
    return pltpu.CompilerParams(dimension_semantics=sem, vmem_limit_bytes=VMEM_LIMIT)


def _layer_norm(y, g, b):
    mu = jnp.mean(y, axis=-1, keepdims=True)
    yc = y - mu
    var = jnp.mean(yc * yc, axis=-1, keepdims=True)
    return yc * lax.rsqrt(var + LN_EPS) * g + b


_NT = (((1,), (1,)), ((), ()))
_TN = (((0,), (0,)), ((), ()))


def _qkv_kernel(x_ref, wqT_ref, wk_ref, wvT_ref, qT_ref, k_ref, vT_ref, *, q_groups):
    xb = x_ref[...].astype(BF16)
    k_ref[...] = jnp.dot(xb, wk_ref[...], preferred_element_type=F32).astype(BF16)
    qT = lax.dot_general(wqT_ref[...], xb, _NT, preferred_element_type=F32).astype(BF16)
    vT = lax.dot_general(wvT_ref[...], xb, _NT, preferred_element_type=F32).astype(BF16)
    _, hv, nblk, hd, bs = vT_ref.shape
    _, hq, _, padded, ql = qT_ref.shape
    pad = padded // hd
    zero = jnp.zeros((hd, bs), BF16)
    for blk in range(nblk):
        cols = slice(blk * bs, (blk + 1) * bs)
        for h in range(hv):
            vT_ref[0, h, blk] = vT[h * hd:(h + 1) * hd, cols]
        for h in range(hq):
            lanes = []
            for g in range(q_groups):
                r0 = (h * q_groups + g) * hd
                parts = [zero] * pad
                parts[h % pad] = qT[r0:r0 + hd, cols]
                lanes.append(jnp.concatenate(parts, axis=0))
            qT_ref[0, h, blk] = lanes[0] if q_groups == 1 else jnp.concatenate(lanes, axis=1)


def qkv_proj(x, wqT, wk, wvT, batch, hq, hv, hd, bs, pad, q_groups, tm=1024):
    n, d = x.shape
    t = n // batch
    tm = min(tm, t)
    nblk = tm // bs
    steps = t // tm
    whole = lambda a: pl.BlockSpec(a.shape, lambda b, i: (0, 0))
    return pl.pallas_call(
        functools.partial(_qkv_kernel, q_groups=q_groups),
        out_shape=(jax.ShapeDtypeStruct((batch, hq, t // bs, pad * hd, q_groups * bs), BF16),
                   jax.ShapeDtypeStruct((n, wk.shape[1]), BF16),
                   jax.ShapeDtypeStruct((batch, hv, t // bs, hd, bs), BF16)),
        grid=(batch, steps),
        in_specs=[pl.BlockSpec((tm, d), lambda b, i: (b * steps + i, 0)), whole(wqT), whole(wk), whole(wvT)],
        out_specs=(pl.BlockSpec((1, hq, nblk, pad * hd, q_groups * bs), lambda b, i: (b, 0, i, 0, 0)),
                   pl.BlockSpec((tm, wk.shape[1]), lambda b, i: (b * steps + i, 0)),
                   pl.BlockSpec((1, hv, nblk, hd, bs), lambda b, i: (b, 0, i, 0, 0))),
        compiler_params=_cparams("parallel", "parallel"),
        name="qkv_proj",
    )(x, wqT, wk, wvT)


def _oT_proj_ln_kernel(oT_ref, w_ref, x_ref, g_ref, b_ref, o_ref, *, bs):
    _, ho, nblk, hd, lanes = oT_ref.shape
    tiles = []
    for blk in range(nblk):
        heads = [oT_ref[0, h, blk, :, g * bs:(g + 1) * bs] for h in range(ho) for g in range(lanes // bs)]
        tiles.append(jnp.concatenate(heads, axis=0))
    aT = tiles[0] if nblk == 1 else jnp.concatenate(tiles, axis=1)
    mix = lax.dot_general(aT, w_ref[...], _TN, preferred_element_type=F32)
    y = DEEPNORM_ALPHA * x_ref[...] + mix
    o_ref[...] = _layer_norm(y, g_ref[...], b_ref[...])


def oT_proj_ln(oT, w, x, g, b, bs, tm=512):
    batch, ho, nb, hd, lanes = oT.shape
    n, d = x.shape
    t = n // batch
    tm = min(tm, t)
    nblk = tm // bs
    steps = t // tm
    row = pl.BlockSpec((1, d), lambda bi, i: (0, 0))
    return pl.pallas_call(
        functools.partial(_oT_proj_ln_kernel, bs=bs),
        out_shape=jax.ShapeDtypeStruct((n, d), F32),
        grid=(batch, steps),
        in_specs=[pl.BlockSpec((1, ho, nblk, hd, lanes), lambda bi, i: (bi, 0, i, 0, 0)),
                  pl.BlockSpec(w.shape, lambda bi, i: (0, 0)),
                  pl.BlockSpec((tm, d), lambda bi, i: (bi * steps + i, 0)), row, row],
        out_specs=pl.BlockSpec((tm, d), lambda bi, i: (bi * steps + i, 0)),
        compiler_params=_cparams("parallel", "parallel"),
        name="proj_ln",
    )(oT, w, x, g.reshape(1, d), b.reshape(1, d))


def _moba_kernel(slope_ref, qblk_ref, kblk_ref, qT_ref, k_ref, vT_ref, o_ref,
                 sel_ref, bias_ref, z_ref, p_ref, acc_ref, st_ref):
    _, hg, nb, hd, bs = vT_ref.shape
    pair = qT_ref.shape[3]
    n_steps = nb * (nb + 1) // 2
    heads = range(hg)
    slopes = [slope_ref[pl.program_id(1) * hg + g] * LOG2E for g in heads]

    key_i = lax.broadcasted_iota(jnp.int32, (bs, bs), 0)
    qry_i = lax.broadcasted_iota(jnp.int32, (bs, bs), 1)
    rel = (qry_i - key_i).astype(F32)
    blk_i = lax.broadcasted_iota(jnp.int32, (nb, bs), 0)
    klanes = [pl.ds((g * hd) // pair * pair, pair) for g in heads]
    kmean = jnp.sum(k_ref[...].astype(F32).reshape(nb, bs, hg * hd), axis=1) * (1.0 / bs)
    km_hi = kmean.astype(BF16)
    km_lo = (kmean - km_hi.astype(F32)).astype(BF16)
    km2 = []
    for g in heads:
        bias = slopes[g] * rel
        bias_ref[g, 0] = bias
        bias_ref[g, 1] = jnp.where(rel >= 0, bias, -NEG_INF)
        lo = (g * hd) // pair * pair
        km2.append(jnp.concatenate([km_hi[:, lo:lo + pair], km_lo[:, lo:lo + pair]], axis=0))

    def select(i, carry):
        past = blk_i < i
        own = jnp.where(blk_i == i, 1.0, 0.0)
        for g in heads:
            g2 = jnp.dot(km2[g], qT_ref[0, g, i], preferred_element_type=F32)
            gate = jnp.where(past, g2[:nb] + g2[nb:], NEG_INF)
            sel = jnp.zeros((nb, bs), F32)
            for _ in range(MOBA_TOPK):
                best = jnp.max(gate, axis=0, keepdims=True)
                first = jnp.min(jnp.where(gate == best, blk_i, nb), axis=0, keepdims=True)
                pick = blk_i == first
                sel = jnp.where(pick, 1.0, sel)
                gate = jnp.where(pick, PICKED, gate)
            sel_ref[g, i] = jnp.where(past, sel, own)
        return carry

    lax.fori_loop(0, nb, select, 0)

    ROW_M, ROW_A, ROW_ZMAX = 0, 1, 3
    ones_rows = jnp.ones((acc_ref.shape[1] - hd, bs), BF16)
    z_ref[...] = jnp.zeros_like(z_ref)
    p_ref[...] = jnp.zeros_like(p_ref)
    acc0 = jnp.where(lax.broadcasted_iota(jnp.int32, acc_ref.shape[1:], 0) < hd, 0.0, 1.0)
    st0 = jnp.where(lax.broadcasted_iota(jnp.int32, (8, bs), 0) == ROW_M, NEG_INF, 1.0)
    for g in heads:
        acc_ref[g] = acc0
        st_ref[g] = st0

    def step(s, par):
        sa = jnp.minimum(s, n_steps - 1)
        sb = jnp.clip(s - 1, 0, n_steps - 1)
        sc = jnp.clip(s - 2, 0, n_steps - 1)
        qa, ka = qblk_ref[sa], kblk_ref[sa]
        qb, kb = qblk_ref[sb], kblk_ref[sb]
        qc, kc = qblk_ref[sc], kblk_ref[sc]
        own_a = (qa == ka).astype(jnp.int32)
        own = qb == kb
        for g in heads:
            k_blk = k_ref[pl.ds(pl.multiple_of(ka * bs, bs), bs), klanes[g]]
            scores = jnp.dot(k_blk, qT_ref[0, g, qa], preferred_element_type=F32)
            z = scores - bias_ref[g, own_a]
            z_ref[par, g] = z
            st_ref[g, pl.ds(ROW_ZMAX + par, 1), :] = jnp.max(z, axis=0, keepdims=True)

            off = slopes[g] * lax.convert_element_type((qb - kb) * bs, F32)
            chosen = jnp.logical_and(sel_ref[g, qb, pl.ds(kb, 1), :] > 0.0, s <= n_steps)
            m_in = jnp.where(own, NEG_INF, st_ref[g, pl.ds(ROW_M, 1), :])
            tile_max = st_ref[g, pl.ds(ROW_ZMAX + 1 - par, 1), :]
            m_new = jnp.where(chosen, jnp.maximum(m_in, tile_max - off), m_in)
            p_f = jnp.exp2(z_ref[1 - par, g] - jnp.where(chosen, m_new + off, -NEG_INF))
            p_ref[1 - par, g] = p_f.astype(BF16)
            st_ref[g, pl.ds(ROW_M, 1), :] = m_new
            st_ref[g, pl.ds(ROW_A + 1 - par, 1), :] = jnp.exp2(m_in - m_new)

            a = st_ref[g, pl.ds(ROW_A + par, 1), :]
            v_aug = jnp.concatenate([vT_ref[0, g, kc], ones_rows], axis=0)
            acc_new = a * acc_ref[g] + jnp.dot(v_aug, p_ref[par, g], preferred_element_type=F32)
            acc_ref[g] = acc_new
            o_ref[0, g, qc] = (acc_new[:hd] / acc_new[hd:hd + 1]).astype(o_ref.dtype)

    def unrolled(t, carry):
        for u in range(MOBA_UNROLL):
            step(MOBA_UNROLL * t + u, u % 2)
        return carry

    lax.fori_loop(0, -(-(n_steps + 2) // MOBA_UNROLL), unrolled, 0)


def moba_attention(qT, k, vT, slopes, heads_per_step=4):
    b, h, nb, hd, bs = vT.shape
    hg = heads_per_step
    steps = [(i, i if t == 0 else t - 1) for i in range(nb) for t in range(i + 1)]
    qblk = jnp.asarray([s[0] for s in steps], jnp.int32)
    kblk = jnp.asarray([s[1] for s in steps], jnp.int32)
    blk = lambda s: pl.BlockSpec((1, hg) + s, lambda bi, hi, *_: (bi, hi, 0, 0, 0))
    return pl.pallas_call(
        _moba_kernel,
        out_shape=jax.ShapeDtypeStruct(vT.shape, BF16),
        grid_spec=pltpu.PrefetchScalarGridSpec(
            num_scalar_prefetch=3,
            grid=(b, h // hg),
            in_specs=[blk((nb, qT.shape[3], bs)),
                      pl.BlockSpec((nb * bs, hg * hd), lambda bi, hi, *_: (bi, hi)),
                      blk((nb, hd, bs))],
            out_specs=blk((nb, hd, bs)),
            scratch_shapes=[pltpu.VMEM((hg, nb, nb, bs), F32),
                            pltpu.VMEM((hg, 2, bs, bs), F32),
                            pltpu.VMEM((2, hg, bs, bs), F32),
                            pltpu.VMEM((2, hg, bs, bs), BF16),
                            pltpu.VMEM((hg, hd + 16, bs), F32),
                            pltpu.VMEM((hg, 8, bs), F32)]),
        compiler_params=_cparams("parallel", "parallel"),
        name="moba_attention",
    )(slopes, qblk, kblk, qT, k, vT)


def _swa_kernel(slope_ref, sink_ref, qT_ref, k_ref, vT_ref, o_ref, pbias_ref, cbias_ref, z_ref):
    nb = qT_ref.shape[2]
    w = vT_ref.shape[4]
    gw = qT_ref.shape[4]
    slope = slope_ref[0]
    sink = sink_ref[0]
    key_i = lax.broadcasted_iota(jnp.int32, (w, gw), 0)
    qry_i = lax.broadcasted_iota(jnp.int32, (w, gw), 1) % w
    d_prev = (w + qry_i - key_i).astype(F32)
    d_cur = (qry_i - key_i).astype(F32)
    pbias_ref[...] = jnp.where(d_prev < w, slope * d_prev, -NEG_INF)
    cbias_ref[...] = jnp.where(d_cur >= 0, slope * d_cur, -NEG_INF)

    def scores(n, slot):
        qT = qT_ref[0, 0, n]
        prev = jnp.maximum(n - 1, 0)
        no_prev = jnp.where(n == 0, -NEG_INF, 0.0)
        k_prev = k_ref[pl.ds(pl.multiple_of(prev * w, w), w), :]
        k_cur = k_ref[pl.ds(pl.multiple_of(n * w, w), w), :]
        z_ref[slot, 0] = jnp.dot(k_prev, qT, preferred_element_type=F32) - pbias_ref[...] - no_prev
        z_ref[slot, 1] = jnp.dot(k_cur, qT, preferred_element_type=F32) - cbias_ref[...]

    def attend(n, slot):
        zp, zc = z_ref[slot, 0], z_ref[slot, 1]
        prev = jnp.maximum(n - 1, 0)
        m = jnp.maximum(jnp.max(zp, axis=0, keepdims=True), jnp.max(zc, axis=0, keepdims=True))
        m = jnp.maximum(m, sink)
        pp = jnp.exp(zp - m)
        pc = jnp.exp(zc - m)
        l = jnp.sum(pp, axis=0, keepdims=True) + jnp.sum(pc, axis=0, keepdims=True) + jnp.exp(sink - m)
        acc = jnp.dot(vT_ref[0, 0, prev], pp.astype(BF16), preferred_element_type=F32)
        acc = acc + jnp.dot(vT_ref[0, 0, n], pc.astype(BF16), preferred_element_type=F32)
        o_ref[0, 0, n] = (acc / l).astype(o_ref.dtype)

    scores(0, 0)

    def two_blocks(t, carry):
        n = 2 * t
        scores(n + 1, 1)
        attend(n, 0)
        scores(jnp.minimum(n + 2, nb - 1), 0)
        attend(n + 1, 1)
        return carry

    lax.fori_loop(0, nb // 2, two_blocks, 0)


def swa_attention(qT, k, vT, slope_lanes, sink_lanes):
    b, kv, nb, hd, w = vT.shape
    gw = qT.shape[4]
    idx = lambda bi, ki: (bi, ki, 0, 0, 0)
    lane = pl.BlockSpec((1, 1, gw), lambda bi, ki: (ki, 0, 0))
    return pl.pallas_call(
        _swa_kernel,
        out_shape=jax.ShapeDtypeStruct((b, kv, nb, hd, gw), BF16),
        grid=(b, kv),
        in_specs=[lane, lane,
                  pl.BlockSpec((1, 1, nb, qT.shape[3], gw), idx),
                  pl.BlockSpec((nb * w, k.shape[1]), lambda bi, ki: (bi, 0)),
                  pl.BlockSpec((1, 1, nb, hd, w), idx)],
        out_specs=pl.BlockSpec((1, 1, nb, hd, gw), idx),
        scratch_shapes=[pltpu.VMEM((w, gw), F32), pltpu.VMEM((w, gw), F32), pltpu.VMEM((2, 2, w, gw), F32)],
        compiler_params=_cparams("parallel", "parallel"),
        name="swa_attention",
    )(slope_lanes, sink_lanes, qT, k, vT)


def _swiglu(xb, wg, wu, wd):
    hg = jnp.dot(xb, wg, preferred_element_type=F32)
    hu = jnp.dot(xb, wu, preferred_element_type=F32)
    hh = hg * jax.nn.sigmoid(hg) * hu
    return jnp.dot(hh.astype(BF16), wd, preferred_element_type=F32)


def _ffn_kernel(x_ref, wg_ref, wu_ref, wd_ref, g_ref, b_ref, o_ref):
    ffn = _swiglu(x_ref[...].astype(BF16), wg_ref[...], wu_ref[...], wd_ref[...])
    y = DEEPNORM_ALPHA * x_ref[...] + ffn
    o_ref[...] = _layer_norm(y, g_ref[...], b_ref[...])


def ffn_ln(x, wg, wu, wd, g, b, tm=512):
    m, d = x.shape
    f = wg.shape[1]
    tm = min(tm, m)
    whole = lambda s: pl.BlockSpec(s, lambda t: (0,) * len(s))
    return pl.pallas_call(
        _ffn_kernel,
        out_shape=jax.ShapeDtypeStruct((m, d), F32),
        grid=(m // tm,),
        in_specs=[pl.BlockSpec((tm, d), lambda t: (t, 0)),
                  whole((d, f)), whole((d, f)), whole((f, d)), whole((1, d)), whole((1, d))],
        out_specs=pl.BlockSpec((tm, d), lambda t: (t, 0)),
        compiler_params=_cparams("parallel"),
        name="ffn_ln",
    )(x, wg, wu, wd, g.reshape(1, d), b.reshape(1, d))


def _router_kernel(x_ref, wrT_ref, gate_ref, rank_ref, count_ref):
    tt = x_ref.shape[0]
    logits = lax.dot_general(wrT_ref[...], x_ref[...], (((1,), (1,)), ((), ())),
                             preferred_element_type=F32, precision=lax.Precision.HIGHEST)
    e_i = lax.broadcasted_iota(jnp.int32, logits.shape, 0)
    work = logits
    vals, picks = [], []
    for _ in range(MOE_TOPK):
        best = jnp.max(work, axis=0, keepdims=True)
        first = jnp.min(jnp.where(work == best, e_i, N_EXPERTS), axis=0, keepdims=True)
        pick = e_i == first
        vals.append(best)
        picks.append(pick)
        work = jnp.where(pick, PICKED, work)
    ex = [jnp.exp(v - vals[0]) for v in vals]
    denom = functools.reduce(lambda a, c: a + c, ex)
    gates = jnp.zeros(logits.shape, F32)
    sel = jnp.zeros(logits.shape, F32)
    for pick, e in zip(picks, ex):
        gates = jnp.where(pick, e / denom, gates)
        sel = jnp.where(pick, 1.0, sel)
    gate_ref[...] = gates
    before = (lax.broadcasted_iota(jnp.int32, (tt, tt), 0) < lax.broadcasted_iota(jnp.int32, (tt, tt), 1))
    rank = jnp.dot(sel.astype(BF16), jnp.where(before, 1.0, 0.0).astype(BF16), preferred_element_type=F32)
    rank_ref[...] = jnp.where(sel > 0.0, rank, -1.0)
    count_ref[0] = jnp.broadcast_to(jnp.sum(sel, axis=1, keepdims=True), count_ref.shape[1:])


def router(x, wrT, tt):
    n, d = x.shape
    ne = wrT.shape[0]
    return pl.pallas_call(
        _router_kernel,
        out_shape=(jax.ShapeDtypeStruct((ne, n), F32), jax.ShapeDtypeStruct((ne, n), F32),
                   jax.ShapeDtypeStruct((n // tt, ne, 128), F32)),
        grid=(n // tt,),
        in_specs=[pl.BlockSpec((tt, d), lambda t: (t, 0)), pl.BlockSpec((ne, d), lambda t: (0, 0))],
        out_specs=(pl.BlockSpec((ne, tt), lambda t: (0, t)), pl.BlockSpec((ne, tt), lambda t: (0, t)),
                   pl.BlockSpec((1, ne, 128), lambda t: (t, 0, 0))),
        compiler_params=_cparams("parallel"),
        name="router",
    )(x, wrT)


MOE_ROWS = 64
MOE_GROUP = 8
MOE_TOKEN_TILE = 1024
MOE_FF_SPLIT = 2


def _moe_kernel(count_ref, x_ref, gate_ref, rank_ref, wg_ref, wu_ref, wd_ref, g_ref, b_ref, o_ref,
                xb_ref, xe_ref, ye_ref):
    t, e, half = pl.program_id(0), pl.program_id(1), pl.program_id(2)
    ne, nh = pl.num_programs(1), pl.num_programs(2)
    tt = x_ref.shape[0]
    rows = MOE_ROWS
    n_chunks = (count_ref[t * ne + e] + rows - 1) // rows
    rank_row = rank_ref[pl.ds(e, 1), :]

    def one_hot(c, k):
        row_i = lax.broadcasted_iota(jnp.int32, (k * rows, tt), 0).astype(F32)
        return rank_row == row_i + lax.convert_element_type(c * rows, F32)

    def row_groups(fn):
        big = MOE_GROUP
        n_big = n_chunks // big

        def body(i, carry):
            fn(i * big, big)
            return carry
        lax.fori_loop(0, n_big, body, 0)
        for k in range(1, big):
            @pl.when(n_chunks - n_big * big == k)
            def _():
                fn(n_big * big, k)

    @pl.when(jnp.logical_and(e == 0, half == 0))
    def _():
        o_ref[...] = jnp.zeros_like(o_ref)
        xb_ref[...] = x_ref[...].astype(BF16)

    @pl.when(half == 0)
    def _():
        def gather(c, k):
            hot = jnp.where(one_hot(c, k), 1.0, 0.0).astype(BF16)
            r0 = pl.multiple_of(c * rows, rows)
            xe_ref[pl.ds(r0, k * rows), :] = jnp.dot(hot, xb_ref[...], preferred_element_type=F32).astype(BF16)
            ye_ref[pl.ds(r0, k * rows), :] = jnp.zeros((k * rows, ye_ref.shape[1]), F32)
        row_groups(gather)

    def expert(c, k):
        r0 = pl.multiple_of(c * rows, rows)
        ye = _swiglu(xe_ref[pl.ds(r0, k * rows), :], wg_ref[0], wu_ref[0], wd_ref[0])
        ye_ref[pl.ds(r0, k * rows), :] += ye
    row_groups(expert)

    @pl.when(half == nh - 1)
    def _():
        gate_row = gate_ref[pl.ds(e, 1), :]

        def scatter(c, k):
            hot = one_hot(c, k)
            r0 = pl.multiple_of(c * rows, rows)
            w_col = jnp.sum(jnp.where(hot, gate_row, 0.0), axis=1, keepdims=True)
            yw = (ye_ref[pl.ds(r0, k * rows), :] * w_col).astype(BF16)
            o_ref[...] += lax.dot_general(jnp.where(hot, 1.0, 0.0).astype(BF16), yw,
                                          (((0,), (0,)), ((), ())), preferred_element_type=F32)
        row_groups(scatter)

    @pl.when(jnp.logical_and(e == ne - 1, half == nh - 1))
    def _():
        y = DEEPNORM_ALPHA * x_ref[...] + o_ref[...]
        o_ref[...] = _layer_norm(y, g_ref[...], b_ref[...])


def moe_ffn_ln(x, gate, rank, count, wg, wu, wd, g, b, tt):
    n, d = x.shape
    ne, fh = wg.shape[0], wg.shape[2] // MOE_FF_SPLIT
    w_cols = pl.BlockSpec((1, d, fh), lambda t, e, h, *_: (e, 0, h))
    w_rows = pl.BlockSpec((1, fh, d), lambda t, e, h, *_: (e, h, 0))
    row = pl.BlockSpec((1, d), lambda t, e, h, *_: (0, 0))
    return pl.pallas_call(
        _moe_kernel,
        out_shape=jax.ShapeDtypeStruct((n, d), F32),
        grid_spec=pltpu.PrefetchScalarGridSpec(
            num_scalar_prefetch=1,
            grid=(n // tt, ne, MOE_FF_SPLIT),
            in_specs=[pl.BlockSpec((tt, d), lambda t, e, h, *_: (t, 0)),
                      pl.BlockSpec((ne, tt), lambda t, e, h, *_: (0, t)),
                      pl.BlockSpec((ne, tt), lambda t, e, h, *_: (0, t)),
                      w_cols, w_cols, w_rows, row, row],
            out_specs=pl.BlockSpec((tt, d), lambda t, e, h, *_: (t, 0)),
            scratch_shapes=[pltpu.VMEM((tt, d), BF16), pltpu.VMEM((tt, d), BF16), pltpu.VMEM((tt, d), F32)]),
        compiler_params=_cparams("parallel", "arbitrary", "arbitrary"),
        name="moe_ffn_ln",
    )(count, x, gate, rank, wg, wu, wd, g.reshape(1, d), b.reshape(1, d))


def _alibi_slopes(n):
    return jnp.exp2(-8.0 * jnp.arange(1, n + 1, dtype=F32) / n)


def kernel(x, w_qkv_a, w_o_a, w_kv_shared, w_q_b, w_o_b, sinks_b, w_gate_d, w_up_d, w_down_d,
           w_router, w_gate_e, w_up_e, w_down_e, ln_gain, ln_bias):
    B, T, D = x.shape
    hd = HEAD_DIM
    H = D // hd
    N = B * T
    xs = x.reshape(N, D)
    q_scale = hd ** -0.5

    HA = w_qkv_a.shape[-1] // 3
    wq, wk, wv = (w_qkv_a[0][:, i * HA:(i + 1) * HA] for i in range(3))
    qT, k, vT = qkv_proj(xs, (wq * (q_scale * LOG2E)).T.astype(BF16), wk.astype(BF16), wv.T.astype(BF16),
                         batch=B, hq=H, hv=H, hd=hd, bs=MOBA_BLOCK, pad=2, q_groups=1)
    oT = moba_attention(qT, k, vT, _alibi_slopes(H))
    xs = oT_proj_ln(oT, w_o_a[0].astype(BF16), xs, ln_gain[0, 0], ln_bias[0, 0], MOBA_BLOCK)

    xs = ffn_ln(xs, w_gate_d[0].astype(BF16), w_up_d[0].astype(BF16), w_down_d[0].astype(BF16),
                ln_gain[0, 1], ln_bias[0, 1])

    KV = N_KV_B
    G = H // KV
    W = WINDOW
    wk, wv = w_kv_shared[:, :KV * hd], w_kv_shared[:, KV * hd:]
    qT, k, vT = qkv_proj(xs, (w_q_b[0] * q_scale).T.astype(BF16), wk.astype(BF16), wv.T.astype(BF16),
                         batch=B, hq=KV, hv=KV, hd=hd, bs=W, pad=KV, q_groups=G)
    slope_lanes = jnp.repeat(_alibi_slopes(H).reshape(KV, G), W, axis=1).reshape(KV, 1, G * W)
    sink_lanes = jnp.repeat(sinks_b[0].astype(F32).reshape(KV, G), W, axis=1).reshape(KV, 1, G * W)
    oT = swa_attention(qT, k, vT, slope_lanes, sink_lanes)
    xs = oT_proj_ln(oT, w_o_b[0].astype(BF16), xs, ln_gain[1, 0], ln_bias[1, 0], W)

    tt = min(MOE_TOKEN_TILE, N)
    gate, rank, count = router(xs, w_router[0].T, tt)
    count = count[:, :, 0].astype(jnp.int32).reshape(-1)
    xs = moe_ffn_ln(xs, gate, rank, count, w_gate_e[0].astype(BF16), w_up_e[0].astype(BF16),
                    w_down_e[0].astype(BF16), ln_gain[1, 1], ln_bias[1, 1], tt)
    return xs.reshape(B, T, D)
```

```python
import functools

import jax
import jax.numpy as jnp
from jax import lax
from jax.experimental import pallas as pl
from jax.experimental.pallas import tpu as pltpu

HEAD_DIM = 64
MOBA_BLOCK = 256
MOBA_TOPK = 3
MOBA_UNROLL = 4
N_KV_B = 2
WINDOW = 128
N_EXPERTS = 8
MOE_TOPK = 2
DEPTH = 2
DEEPNORM_ALPHA = (2 * DEPTH) ** 0.25
LN_EPS = 1e-5
NEG_INF = -1e30
PICKED = -3e38
LOG2E = 1.4426950408889634
VMEM_LIMIT = 56 * 1024 * 1024

F32 = jnp.float32
BF16 = jnp.bfloat16


def _cparams(*sem):
    return pltpu.CompilerParams(dimension_semantics=sem, vmem_limit_bytes=VMEM_LIMIT)


def _layer_norm(y, g, b):
    mu = jnp.mean(y, axis=-1, keepdims=True)
    yc = y - mu
    var = jnp.mean(yc * yc, axis=-1, keepdims=True)
    return yc * lax.rsqrt(var + LN_EPS) * g + b


_NT = (((1,), (1,)), ((), ()))
_TN = (((0,), (0,)), ((), ()))


def _qkv_kernel(x_ref, wqT_ref, wk_ref, wvT_ref, qT_ref, k_ref, vT_ref, *, q_groups):
    xb = x_ref[...].astype(BF16)
    k_ref[...] = jnp.dot(xb, wk_ref[...], preferred_element_type=F32).astype(BF16)
    qT = lax.dot_general(wqT_ref[...], xb, _NT, preferred_element_type=F32).astype(BF16)
    vT = lax.dot_general(wvT_ref[...], xb, _NT, preferred_element_type=F32).astype(BF16)
    _, hv, nblk, hd, bs = vT_ref.shape
    _, hq, _, padded, ql = qT_ref.shape
    pad = padded // hd
    zero = jnp.zeros((hd, bs), BF16)
    for blk in range(nblk):
        cols = slice(blk * bs, (blk + 1) * bs)
        for h in range(hv):
            vT_ref[0, h, blk] = vT[h * hd:(h + 1) * hd, cols]
        for h in range(hq):
            lanes = []
            for g in range(q_groups):
                r0 = (h * q_groups + g) * hd
                parts = [zero] * pad
                parts[h % pad] = qT[r0:r0 + hd, cols]
                lanes.append(jnp.concatenate(parts, axis=0))
            qT_ref[0, h, blk] = lanes[0] if q_groups == 1 else jnp.concatenate(lanes, axis=1)


def qkv_proj(x, wqT, wk, wvT, batch, hq, hv, hd, bs, pad, q_groups, tm=1024):
    n, d = x.shape
    t = n // batch
    tm = min(tm, t)
    nblk = tm // bs
    steps = t // tm
    whole = lambda a: pl.BlockSpec(a.shape, lambda b, i: (0, 0))
    return pl.pallas_call(
        functools.partial(_qkv_kernel, q_groups=q_groups),
        out_shape=(jax.ShapeDtypeStruct((batch, hq, t // bs, pad * hd, q_groups * bs), BF16),
                   jax.ShapeDtypeStruct((n, wk.shape[1]), BF16),
                   jax.ShapeDtypeStruct((batch, hv, t // bs, hd, bs), BF16)),
        grid=(batch, steps),
        in_specs=[pl.BlockSpec((tm, d), lambda b, i: (b * steps + i, 0)), whole(wqT), whole(wk), whole(wvT)],
        out_specs=(pl.BlockSpec((1, hq, nblk, pad * hd, q_groups * bs), lambda b, i: (b, 0, i, 0, 0)),
                   pl.BlockSpec((tm, wk.shape[1]), lambda b, i: (b * steps + i, 0)),
                   pl.BlockSpec((1, hv, nblk, hd, bs), lambda b, i: (b, 0, i, 0, 0))),
        compiler_params=_cparams("parallel", "parallel"),
        name="qkv_proj",
    )(x, wqT, wk, wvT)


def _oT_proj_ln_kernel(oT_ref, w_ref, x_ref, g_ref, b_ref, o_ref, *, bs):
    _, ho, nblk, hd, lanes = oT_ref.shape
    tiles = []
    for blk in range(nblk):
        heads = [oT_ref[0, h, blk, :, g * bs:(g + 1) * bs] for h in range(ho) for g in range(lanes // bs)]
        tiles.append(jnp.concatenate(heads, axis=0))
    aT = tiles[0] if nblk == 1 else jnp.concatenate(tiles, axis=1)
    mix = lax.dot_general(aT, w_ref[...], _TN, preferred_element_type=F32)
    y = DEEPNORM_ALPHA * x_ref[...] + mix
    o_ref[...] = _layer_norm(y, g_ref[...], b_ref[...])


def oT_proj_ln(oT, w, x, g, b, bs, tm=512):
    batch, ho, nb, hd, lanes = oT.shape
    n, d = x.shape
    t = n // batch
    tm = min(tm, t)
    nblk = tm // bs
    steps = t // tm
    row = pl.BlockSpec((1, d), lambda bi, i: (0, 0))
    return pl.pallas_call(
        functools.partial(_oT_proj_ln_kernel, bs=bs),
        out_shape=jax.ShapeDtypeStruct((n, d), F32),
        grid=(batch, steps),
        in_specs=[pl.BlockSpec((1, ho, nblk, hd, lanes), lambda bi, i: (bi, 0, i, 0, 0)),
                  pl.BlockSpec(w.shape, lambda bi, i: (0, 0)),
                  pl.BlockSpec((tm, d), lambda bi, i: (bi * steps + i, 0)), row, row],
        out_specs=pl.BlockSpec((tm, d), lambda bi, i: (bi * steps + i, 0)),
        compiler_params=_cparams("parallel", "parallel"),
        name="proj_ln",
    )(oT, w, x, g.reshape(1, d), b.reshape(1, d))


def _moba_kernel(slope_ref, qblk_ref, kblk_ref, qT_ref, k_ref, vT_ref, o_ref,
                 sel_ref, bias_ref, z_ref, p_ref, acc_ref, st_ref):
    _, hg, nb, hd, bs = vT_ref.shape
    pair = qT_ref.shape[3]
    n_steps = nb * (nb + 1) // 2
    heads = range(hg)
    slopes = [slope_ref[pl.program_id(1) * hg + g] * LOG2E for g in heads]

    key_i = lax.broadcasted_iota(jnp.int32, (bs, bs), 0)
    qry_i = lax.broadcasted_iota(jnp.int32, (bs, bs), 1)
    rel = (qry_i - key_i).astype(F32)
    blk_i = lax.broadcasted_iota(jnp.int32, (nb, bs), 0)
    klanes = [pl.ds((g * hd) // pair * pair, pair) for g in heads]
    kmean = jnp.sum(k_ref[...].astype(F32).reshape(nb, bs, hg * hd), axis=1) * (1.0 / bs)
    km_hi = kmean.astype(BF16)
    km_lo = (kmean - km_hi.astype(F32)).astype(BF16)
    km2 = []
    for g in heads:
        bias = slopes[g] * rel
        bias_ref[g, 0] = bias
        bias_ref[g, 1] = jnp.where(rel >= 0, bias, -NEG_INF)
        lo = (g * hd) // pair * pair
        km2.append(jnp.concatenate([km_hi[:, lo:lo + pair], km_lo[:, lo:lo + pair]], axis=0))

    def select(i, carry):
        past = blk_i < i
        own = jnp.where(blk_i == i, 1.0, 0.0)
        for g in heads:
            g2 = jnp.dot(km2[g], qT_ref[0, g, i], preferred_element_type=F32)
            gate = jnp.where(past, g2[:nb] + g2[nb:], NEG_INF)
            sel = jnp.zeros((nb, bs), F32)
            for _ in range(MOBA_TOPK):
                best = jnp.max(gate, axis=0, keepdims=True)
                first = jnp.min(jnp.where(gate == best, blk_i, nb), axis=0, keepdims=True)
                pick = blk_i == first
                sel = jnp.where(pick, 1.0, sel)
                gate = jnp.where(pick, PICKED, gate)
            sel_ref[g, i] = jnp.where(past, sel, own)
        return carry

    lax.fori_loop(0, nb, select, 0)

    ROW_M, ROW_A, ROW_ZMAX = 0, 1, 3
    ones_rows = jnp.ones((acc_ref.shape[1] - hd, bs), BF16)
    z_ref[...] = jnp.zeros_like(z_ref)
    p_ref[...] = jnp.zeros_like(p_ref)
    acc0 = jnp.where(lax.broadcasted_iota(jnp.int32, acc_ref.shape[1:], 0) < hd, 0.0, 1.0)
    st0 = jnp.where(lax.broadcasted_iota(jnp.int32, (8, bs), 0) == ROW_M, NEG_INF, 1.0)
    for g in heads:
        acc_ref[g] = acc0
        st_ref[g] = st0

    def step(s, par):
        sa = jnp.minimum(s, n_steps - 1)
        sb = jnp.clip(s - 1, 0, n_steps - 1)
        sc = jnp.clip(s - 2, 0, n_steps - 1)
        qa, ka = qblk_ref[sa], kblk_ref[sa]
        qb, kb = qblk_ref[sb], kblk_ref[sb]
        qc, kc = qblk_ref[sc], kblk_ref[sc]
        own_a = (qa == ka).astype(jnp.int32)
        own = qb == kb
        for g in heads:
            k_blk = k_ref[pl.ds(pl.multiple_of(ka * bs, bs), bs), klanes[g]]
            scores = jnp.dot(k_blk, qT_ref[0, g, qa], preferred_element_type=F32)
            z = scores - bias_ref[g, own_a]
            z_ref[par, g] = z
            st_ref[g, pl.ds(ROW_ZMAX + par, 1), :] = jnp.max(z, axis=0, keepdims=True)

            off = slopes[g] * lax.convert_element_type((qb - kb) * bs, F32)
            chosen = jnp.logical_and(sel_ref[g, qb, pl.ds(kb, 1), :] > 0.0, s <= n_steps)
            m_in = jnp.where(own, NEG_INF, st_ref[g, pl.ds(ROW_M, 1), :])
            tile_max = st_ref[g, pl.ds(ROW_ZMAX + 1 - par, 1), :]
            m_new = jnp.where(chosen, jnp.maximum(m_in, tile_max - off), m_in)
            p_f = jnp.exp2(z_ref[1 - par, g] - jnp.where(chosen, m_new + off, -NEG_INF))
            p_ref[1 - par, g] = p_f.astype(BF16)
            st_ref[g, pl.ds(ROW_M, 1), :] = m_new
            st_ref[g, pl.ds(ROW_A + 1 - par, 1), :] = jnp.exp2(m_in - m_new)

            a = st_ref[g, pl.ds(ROW_A + par, 1), :]
            v_aug = jnp.concatenate([vT_ref[0, g, kc], ones_rows], axis=0)
            acc_new = a * acc_ref[g] + jnp.dot(v_aug, p_ref[par, g], preferred_element_type=F32)
            acc_ref[g] = acc_new
            o_ref[0, g, qc] = (acc_new[:hd] / acc_new[hd:hd + 1]).astype(o_ref.dtype)

    def unrolled(t, carry):
        for u in range(MOBA_UNROLL):
            step(MOBA_UNROLL * t + u, u % 2)
        return carry

    lax.fori_loop(0, -(-(n_steps + 2) // MOBA_UNROLL), unrolled, 0)


def moba_attention(qT, k, vT, slopes, heads_per_step=4):
    b, h, nb, hd, bs = vT.shape
    hg = heads_per_step
    steps = [(i, i if t == 0 else t - 1) for i in range(nb) for t in range(i + 1)]
    qblk = jnp.asarray([s[0] for s in steps], jnp.int32)
    kblk = jnp.asarray([s[1] for s in steps], jnp.int32)
    blk = lambda s: pl.BlockSpec((1, hg) + s, lambda bi, hi, *_: (bi, hi, 0, 0, 0))
    return pl.pallas_call(
        _moba_kernel,
        out_shape=jax.ShapeDtypeStruct(vT.shape, BF16),
        grid_spec=pltpu.PrefetchScalarGridSpec(
            num_scalar_prefetch=3,
            grid=(b, h // hg),
            in_specs=[blk((nb, qT.shape[3], bs)),
                      pl.BlockSpec((nb * bs, hg * hd), lambda bi, hi, *_: (bi, hi)),
                      blk((nb, hd, bs))],
            out_specs=blk((nb, hd, bs)),
            scratch_shapes=[pltpu.VMEM((hg, nb, nb, bs), F32),
                            pltpu.VMEM((hg, 2, bs, bs), F32),
                            pltpu.VMEM((2, hg, bs, bs), F32),
                            pltpu.VMEM((2, hg, bs, bs), BF16),
                            pltpu.VMEM((hg, hd + 16, bs), F32),
                            pltpu.VMEM((hg, 8, bs), F32)]),
        compiler_params=_cparams("parallel", "parallel"),
        name="moba_attention",
    )(slopes, qblk, kblk, qT, k, vT)


def _swa_kernel(slope_ref, sink_ref, qT_ref, k_ref, vT_ref, o_ref, pbias_ref, cbias_ref, z_ref):
    nb = qT_ref.shape[2]
    w = vT_ref.shape[4]
    gw = qT_ref.shape[4]
    slope = slope_ref[0]
    sink = sink_ref[0]
    key_i = lax.broadcasted_iota(jnp.int32, (w, gw), 0)
    qry_i = lax.broadcasted_iota(jnp.int32, (w, gw), 1) % w
    d_prev = (w + qry_i - key_i).astype(F32)
    d_cur = (qry_i - key_i).astype(F32)
    pbias_ref[...] = jnp.where(d_prev < w, slope * d_prev, -NEG_INF)
    cbias_ref[...] = jnp.where(d_cur >= 0, slope * d_cur, -NEG_INF)

    def scores(n, slot):
        qT = qT_ref[0, 0, n]
        prev = jnp.maximum(n - 1, 0)
        no_prev = jnp.where(n == 0, -NEG_INF, 0.0)
        k_prev = k_ref[pl.ds(pl.multiple_of(prev * w, w), w), :]
        k_cur = k_ref[pl.ds(pl.multiple_of(n * w, w), w), :]
        z_ref[slot, 0] = jnp.dot(k_prev, qT, preferred_element_type=F32) - pbias_ref[...] - no_prev
        z_ref[slot, 1] = jnp.dot(k_cur, qT, preferred_element_type=F32) - cbias_ref[...]

    def attend(n, slot):
        zp, zc = z_ref[slot, 0], z_ref[slot, 1]
        prev = jnp.maximum(n - 1, 0)
        m = jnp.maximum(jnp.max(zp, axis=0, keepdims=True), jnp.max(zc, axis=0, keepdims=True))
        m = jnp.maximum(m, sink)
        pp = jnp.exp(zp - m)
        pc = jnp.exp(zc - m)
        l = jnp.sum(pp, axis=0, keepdims=True) + jnp.sum(pc, axis=0, keepdims=True) + jnp.exp(sink - m)
        acc = jnp.dot(vT_ref[0, 0, prev], pp.astype(BF16), preferred_element_type=F32)
        acc = acc + jnp.dot(vT_ref[0, 0, n], pc.astype(BF16), preferred_element_type=F32)
        o_ref[0, 0, n] = (acc / l).astype(o_ref.dtype)

    scores(0, 0)

    def two_blocks(t, carry):
        n = 2 * t
        scores(n + 1, 1)
        attend(n, 0)
        scores(jnp.minimum(n + 2, nb - 1), 0)
        attend(n + 1, 1)
        return carry

    lax.fori_loop(0, nb // 2, two_blocks, 0)


def swa_attention(qT, k, vT, slope_lanes, sink_lanes):
    b, kv, nb, hd, w = vT.shape
    gw = qT.shape[4]
    idx = lambda bi, ki: (bi, ki, 0, 0, 0)
    lane = pl.BlockSpec((1, 1, gw), lambda bi, ki: (ki, 0, 0))
    return pl.pallas_call(
        _swa_kernel,
        out_shape=jax.ShapeDtypeStruct((b, kv, nb, hd, gw), BF16),
        grid=(b, kv),
        in_specs=[lane, lane,
                  pl.BlockSpec((1, 1, nb, qT.shape[3], gw), idx),
                  pl.BlockSpec((nb * w, k.shape[1]), lambda bi, ki: (bi, 0)),
                  pl.BlockSpec((1, 1, nb, hd, w), idx)],
        out_specs=pl.BlockSpec((1, 1, nb, hd, gw), idx),
        scratch_shapes=[pltpu.VMEM((w, gw), F32), pltpu.VMEM((w, gw), F32), pltpu.VMEM((2, 2, w, gw), F32)],
        compiler_params=_cparams("parallel", "parallel"),
        name="swa_attention",
    )(slope_lanes, sink_lanes, qT, k, vT)


def _swiglu(xb, wg, wu, wd):
    hg = jnp.dot(xb, wg, preferred_element_type=F32)
    hu = jnp.dot(xb, wu, preferred_element_type=F32)
    hh = hg * jax.nn.sigmoid(hg) * hu
    return jnp.dot(hh.astype(BF16), wd, preferred_element_type=F32)


def _ffn_kernel(x_ref, wg_ref, wu_ref, wd_ref, g_ref, b_ref, o_ref):
    ffn = _swiglu(x_ref[...].astype(BF16), wg_ref[...], wu_ref[...], wd_ref[...])
    y = DEEPNORM_ALPHA * x_ref[...] + ffn
    o_ref[...] = _layer_norm(y, g_ref[...], b_ref[...])


def ffn_ln(x, wg, wu, wd, g, b, tm=512):
    m, d = x.shape
    f = wg.shape[1]
    tm = min(tm, m)
    whole = lambda s: pl.BlockSpec(s, lambda t: (0,) * len(s))
    return pl.pallas_call(
        _ffn_kernel,
        out_shape=jax.ShapeDtypeStruct((m, d), F32),
        grid=(m // tm,),
        in_specs=[pl.BlockSpec((tm, d), lambda t: (t, 0)),
                  whole((d, f)), whole((d, f)), whole((f, d)), whole((1, d)), whole((1, d))],
        out_specs=pl.BlockSpec((tm, d), lambda t: (t, 0)),
        compiler_params=_cparams("parallel"),
        name="ffn_ln",
    )(x, wg, wu, wd, g.reshape(1, d), b.reshape(1, d))


def _router_kernel(x_ref, wrT_ref, gate_ref, rank_ref, count_ref):
    tt = x_ref.shape[0]
    logits = lax.dot_general(wrT_ref[...], x_ref[...], (((1,), (1,)), ((), ())),
                             preferred_element_type=F32, precision=lax.Precision.HIGHEST)
    e_i = lax.broadcasted_iota(jnp.int32, logits.shape, 0)
    work = logits
    vals, picks = [], []
    for _ in range(MOE_TOPK):
        best = jnp.max(work, axis=0, keepdims=True)
        first = jnp.min(jnp.where(work == best, e_i, N_EXPERTS), axis=0, keepdims=True)
        pick = e_i == first
        vals.append(best)
        picks.append(pick)
        work = jnp.where(pick, PICKED, work)
    ex = [jnp.exp(v - vals[0]) for v in vals]
    denom = functools.reduce(lambda a, c: a + c, ex)
    gates = jnp.zeros(logits.shape, F32)
    sel = jnp.zeros(logits.shape, F32)
    for pick, e in zip(picks, ex):
        gates = jnp.where(pick, e / denom, gates)
        sel = jnp.where(pick, 1.0, sel)
    gate_ref[...] = gates
    before = (lax.broadcasted_iota(jnp.int32, (tt, tt), 0) < lax.broadcasted_iota(jnp.int32, (tt, tt), 1))
    rank = jnp.dot(sel.astype(BF16), jnp.where(before, 1.0, 0.0).astype(BF16), preferred_element_type=F32)
    rank_ref[...] = jnp.where(sel > 0.0, rank, -1.0)
    count_ref[0] = jnp.broadcast_to(jnp.sum(sel, axis=1, keepdims=True), count_ref.shape[1:])


def router(x, wrT, tt):
    n, d = x.shape
    ne = wrT.shape[0]
    return pl.pallas_call(
        _router_kernel,
        out_shape=(jax.ShapeDtypeStruct((ne, n), F32), jax.ShapeDtypeStruct((ne, n), F32),
                   jax.ShapeDtypeStruct((n // tt, ne, 128), F32)),
        grid=(n // tt,),
        in_specs=[pl.BlockSpec((tt, d), lambda t: (t, 0)), pl.BlockSpec((ne, d), lambda t: (0, 0))],
        out_specs=(pl.BlockSpec((ne, tt), lambda t: (0, t)), pl.BlockSpec((ne, tt), lambda t: (0, t)),
                   pl.BlockSpec((1, ne, 128), lambda t: (t, 0, 0))),
        compiler_params=_cparams("parallel"),
        name="router",
    )(x, wrT)


MOE_BIG = 512
MOE_SMALL = (128, 256, 320)
MOE_TOKEN_TILE = 1024
MOE_FF_SPLIT = 2


def _moe_kernel(count_ref, x_ref, gate_ref, rank_ref, wg_ref, wu_ref, wd_ref, g_ref, b_ref, o_ref,
                xb_ref, xe_ref, ye_ref):
    t, e, half = pl.program_id(0), pl.program_id(1), pl.program_id(2)
    ne, nh = pl.num_programs(1), pl.num_programs(2)
    tt = x_ref.shape[0]
    count = count_ref[t * ne + e]
    n_big = count // MOE_BIG
    rest = count - n_big * MOE_BIG
    rank_row = rank_ref[pl.ds(e, 1), :]

    def one_hot(r0, m):
        row_i = lax.broadcasted_iota(jnp.int32, (m, tt), 0).astype(F32)
        return rank_row == row_i + lax.convert_element_type(r0, F32)

    def row_groups(fn):
        def body(i, carry):
            fn(pl.multiple_of(i * MOE_BIG, MOE_BIG), MOE_BIG)
            return carry
        lax.fori_loop(0, n_big + (rest > MOE_SMALL[-1]).astype(jnp.int32), body, 0)
        lo = 0
        for m in MOE_SMALL:
            @pl.when(jnp.logical_and(rest > lo, rest <= m))
            def _():
                fn(pl.multiple_of(n_big * MOE_BIG, MOE_BIG), m)
            lo = m

    @pl.when(jnp.logical_and(e == 0, half == 0))
    def _():
        o_ref[...] = jnp.zeros_like(o_ref)
        xb_ref[...] = x_ref[...].astype(BF16)

    @pl.when(half == 0)
    def _():
        def gather(r0, m):
            hot = jnp.where(one_hot(r0, m), 1.0, 0.0).astype(BF16)
            xe_ref[pl.ds(r0, m), :] = jnp.dot(hot, xb_ref[...], preferred_element_type=F32).astype(BF16)
            ye_ref[pl.ds(r0, m), :] = jnp.zeros((m, ye_ref.shape[1]), F32)
        row_groups(gather)

    def expert(r0, m):
        ye_ref[pl.ds(r0, m), :] += _swiglu(xe_ref[pl.ds(r0, m), :], wg_ref[0], wu_ref[0], wd_ref[0])
    row_groups(expert)

    @pl.when(half == nh - 1)
    def _():
        gate_row = gate_ref[pl.ds(e, 1), :]

        def scatter(r0, m):
            hot = one_hot(r0, m)
            w_col = jnp.sum(jnp.where(hot, gate_row, 0.0), axis=1, keepdims=True)
            yw = (ye_ref[pl.ds(r0, m), :] * w_col).astype(BF16)
            o_ref[...] += lax.dot_general(jnp.where(hot, 1.0, 0.0).astype(BF16), yw,
                                          (((0,), (0,)), ((), ())), preferred_element_type=F32)
        row_groups(scatter)

    @pl.when(jnp.logical_and(e == ne - 1, half == nh - 1))
    def _():
        y = DEEPNORM_ALPHA * x_ref[...] + o_ref[...]
        o_ref[...] = _layer_norm(y, g_ref[...], b_ref[...])


def moe_ffn_ln(x, gate, rank, count, wg, wu, wd, g, b, tt):
    n, d = x.shape
    ne, fh = wg.shape[0], wg.shape[2] // MOE_FF_SPLIT
    w_cols = pl.BlockSpec((1, d, fh), lambda t, e, h, *_: (e, 0, h))
    w_rows = pl.BlockSpec((1, fh, d), lambda t, e, h, *_: (e, h, 0))
    row = pl.BlockSpec((1, d), lambda t, e, h, *_: (0, 0))
    return pl.pallas_call(
        _moe_kernel,
        out_shape=jax.ShapeDtypeStruct((n, d), F32),
        grid_spec=pltpu.PrefetchScalarGridSpec(
            num_scalar_prefetch=1,
            grid=(n // tt, ne, MOE_FF_SPLIT),
            in_specs=[pl.BlockSpec((tt, d), lambda t, e, h, *_: (t, 0)),
                      pl.BlockSpec((ne, tt), lambda t, e, h, *_: (0, t)),
                      pl.BlockSpec((ne, tt), lambda t, e, h, *_: (0, t)),
                      w_cols, w_cols, w_rows, row, row],
            out_specs=pl.BlockSpec((tt, d), lambda t, e, h, *_: (t, 0)),
            scratch_shapes=[pltpu.VMEM((tt, d), BF16), pltpu.VMEM((tt, d), BF16), pltpu.VMEM((tt, d), F32)]),
        compiler_params=_cparams("parallel", "arbitrary", "arbitrary"),
        name="moe_ffn_ln",
    )(count, x, gate, rank, wg, wu, wd, g.reshape(1, d), b.reshape(1, d))


def _alibi_slopes(n):
    return jnp.exp2(-8.0 * jnp.arange(1, n + 1, dtype=F32) / n)


def kernel(x, w_qkv_a, w_o_a, w_kv_shared, w_q_b, w_o_b, sinks_b, w_gate_d, w_up_d, w_down_d,
           w_router, w_gate_e, w_up_e, w_down_e, ln_gain, ln_bias):
    B, T, D = x.shape
    hd = HEAD_DIM
    H = D // hd
    N = B * T
    xs = x.reshape(N, D)
    q_scale = hd ** -0.5

    HA = w_qkv_a.shape[-1] // 3
    wq, wk, wv = (w_qkv_a[0][:, i * HA:(i + 1) * HA] for i in range(3))
    qT, k, vT = qkv_proj(xs, (wq * (q_scale * LOG2E)).T.astype(BF16), wk.astype(BF16), wv.T.astype(BF16),
                         batch=B, hq=H, hv=H, hd=hd, bs=MOBA_BLOCK, pad=2, q_groups=1)
    oT = moba_attention(qT, k, vT, _alibi_slopes(H))
    xs = oT_proj_ln(oT, w_o_a[0].astype(BF16), xs, ln_gain[0, 0], ln_bias[0, 0], MOBA_BLOCK)

    xs = ffn_ln(xs, w_gate_d[0].astype(BF16), w_up_d[0].astype(BF16), w_down_d[0].astype(BF16),
                ln_gain[0, 1], ln_bias[0, 1])

    KV = N_KV_B
    G = H // KV
    W = WINDOW
    wk, wv = w_kv_shared[:, :KV * hd], w_kv_shared[:, KV * hd:]
    qT, k, vT = qkv_proj(xs, (w_q_b[0] * q_scale).T.astype(BF16), wk.astype(BF16), wv.T.astype(BF16),
                         batch=B, hq=KV, hv=KV, hd=hd, bs=W, pad=KV, q_groups=G)
    slope_lanes = jnp.repeat(_alibi_slopes(H).reshape(KV, G), W, axis=1).reshape(KV, 1, G * W)
    sink_lanes = jnp.repeat(sinks_b[0].astype(F32).reshape(KV, G), W, axis=1).reshape(KV, 1, G * W)
    oT = swa_attention(qT, k, vT, slope_lanes, sink_lanes)
    xs = oT_proj_ln(oT, w_o_b[0].astype(BF16), xs, ln_gain[1, 0], ln_bias[1, 0], W)

    tt = min(MOE_TOKEN_TILE, N)
    gate, rank, count = router(xs, w_router[0].T, tt)
    count = count[:, :, 0].astype(jnp.int32).reshape(-1)
    xs = moe_ffn_ln(xs, gate, rank, count, w_gate_e[0].astype(BF16), w_up_e[0].astype(BF16),
                    w_down_e[0].astype(BF16), ln_gain[1, 1], ln_bias[1, 1], tt)
    return xs.reshape(B, T, D)
```

```python
import functools

import jax
import jax.numpy as jnp
from jax import lax
from jax.experimental import pallas as pl
from jax.experimental.pallas import tpu as pltpu

HEAD_DIM = 64
MOBA_BLOCK = 256
MOBA_TOPK = 3
MOBA_UNROLL = 4
PROJ_ROWS = 256
N_KV_B = 2
WINDOW = 128
N_EXPERTS = 8
MOE_TOPK = 2
DEPTH = 2
DEEPNORM_ALPHA = (2 * DEPTH) ** 0.25
LN_EPS = 1e-5
NEG_INF = -1e30
PICKED = -3e38
LOG2E = 1.4426950408889634
VMEM_LIMIT = 56 * 1024 * 1024

F32 = jnp.float32
BF16 = jnp.bfloat16


def _cparams(*sem):
    return pltpu.CompilerParams(dimension_semantics=sem, vmem_limit_bytes=VMEM_LIMIT)


def _layer_norm(y, g, b):
    mu = jnp.mean(y, axis=-1, keepdims=True)
    yc = y - mu
    var = jnp.mean(yc * yc, axis=-1, keepdims=True)
    return yc * lax.rsqrt(var + LN_EPS) * g + b


_NT = (((1,), (1,)), ((), ()))
_TN = (((0,), (0,)), ((), ()))


def _qkv_kernel(x_ref, wqT_ref, wk_ref, wvT_ref, qT_ref, k_ref, vT_ref, *, q_groups):
    xb = x_ref[...].astype(BF16)
    k_ref[...] = jnp.dot(xb, wk_ref[...], preferred_element_type=F32).astype(BF16)
    qT = lax.dot_general(wqT_ref[...], xb, _NT, preferred_element_type=F32).astype(BF16)
    vT = lax.dot_general(wvT_ref[...], xb, _NT, preferred_element_type=F32).astype(BF16)
    _, hv, nblk, hd, bs = vT_ref.shape
    _, hq, _, padded, ql = qT_ref.shape
    pad = padded // hd
    zero = jnp.zeros((hd, bs), BF16)
    for blk in range(nblk):
        cols = slice(blk * bs, (blk + 1) * bs)
        for h in range(hv):
            vT_ref[0, h, blk] = vT[h * hd:(h + 1) * hd, cols]
        for h in range(hq):
            lanes = []
            for g in range(q_groups):
                r0 = (h * q_groups + g) * hd
                parts = [zero] * pad
                parts[h % pad] = qT[r0:r0 + hd, cols]
                lanes.append(jnp.concatenate(parts, axis=0))
            qT_ref[0, h, blk] = lanes[0] if q_groups == 1 else jnp.concatenate(lanes, axis=1)


def qkv_proj(x, wqT, wk, wvT, batch, hq, hv, hd, bs, pad, q_groups, tm=1024):
    n, d = x.shape
    t = n // batch
    tm = min(tm, t)
    nblk = tm // bs
    steps = t // tm
    whole = lambda a: pl.BlockSpec(a.shape, lambda b, i: (0, 0))
    return pl.pallas_call(
        functools.partial(_qkv_kernel, q_groups=q_groups),
        out_shape=(jax.ShapeDtypeStruct((batch, hq, t // bs, pad * hd, q_groups * bs), BF16),
                   jax.ShapeDtypeStruct((n, wk.shape[1]), BF16),
                   jax.ShapeDtypeStruct((batch, hv, t // bs, hd, bs), BF16)),
        grid=(batch, steps),
        in_specs=[pl.BlockSpec((tm, d), lambda b, i: (b * steps + i, 0)), whole(wqT), whole(wk), whole(wvT)],
        out_specs=(pl.BlockSpec((1, hq, nblk, pad * hd, q_groups * bs), lambda b, i: (b, 0, i, 0, 0)),
                   pl.BlockSpec((tm, wk.shape[1]), lambda b, i: (b * steps + i, 0)),
                   pl.BlockSpec((1, hv, nblk, hd, bs), lambda b, i: (b, 0, i, 0, 0))),
        compiler_params=_cparams("parallel", "parallel"),
        name="qkv_proj",
    )(x, wqT, wk, wvT)


def _oT_proj_ln_kernel(oT_ref, w_ref, x_ref, g_ref, b_ref, o_ref, *, bs):
    _, ho, nblk, hd, lanes = oT_ref.shape
    per = max(1, PROJ_ROWS // bs)
    mixes = []
    for first in range(0, nblk, per):
        tiles = []
        for blk in range(first, first + per):
            heads = [oT_ref[0, h, blk, :, g * bs:(g + 1) * bs] for h in range(ho) for g in range(lanes // bs)]
            tiles.append(jnp.concatenate(heads, axis=0))
        aT = tiles[0] if per == 1 else jnp.concatenate(tiles, axis=1)
        mixes.append(lax.dot_general(aT, w_ref[...], _TN, preferred_element_type=F32))
    for i, mix in enumerate(mixes):
        rows = pl.ds(i * per * bs, per * bs)
        y = DEEPNORM_ALPHA * x_ref[rows, :] + mix
        o_ref[rows, :] = _layer_norm(y, g_ref[...], b_ref[...])


def oT_proj_ln(oT, w, x, g, b, bs, tm=1024):
    batch, ho, nb, hd, lanes = oT.shape
    n, d = x.shape
    t = n // batch
    tm = min(tm, t)
    nblk = tm // bs
    steps = t // tm
    row = pl.BlockSpec((1, d), lambda bi, i: (0, 0))
    return pl.pallas_call(
        functools.partial(_oT_proj_ln_kernel, bs=bs),
        out_shape=jax.ShapeDtypeStruct((n, d), F32),
        grid=(batch, steps),
        in_specs=[pl.BlockSpec((1, ho, nblk, hd, lanes), lambda bi, i: (bi, 0, i, 0, 0)),
                  pl.BlockSpec(w.shape, lambda bi, i: (0, 0)),
                  pl.BlockSpec((tm, d), lambda bi, i: (bi * steps + i, 0)), row, row],
        out_specs=pl.BlockSpec((tm, d), lambda bi, i: (bi * steps + i, 0)),
        compiler_params=_cparams("parallel", "parallel"),
        name="proj_ln",
    )(oT, w, x, g.reshape(1, d), b.reshape(1, d))


def _moba_kernel(slope_ref, qblk_ref, kblk_ref, qT_ref, k_ref, vT_ref, o_ref,
                 sel_ref, bias_ref, z_ref, p_ref, acc_ref, st_ref):
    _, hg, nb, hd, bs = vT_ref.shape
    pair = qT_ref.shape[3]
    n_steps = nb * (nb + 1) // 2
    heads = range(hg)
    slopes = [slope_ref[pl.program_id(1) * hg + g] * LOG2E for g in heads]

    key_i = lax.broadcasted_iota(jnp.int32, (bs, bs), 0)
    qry_i = lax.broadcasted_iota(jnp.int32, (bs, bs), 1)
    rel = (qry_i - key_i).astype(F32)
    blk_i = lax.broadcasted_iota(jnp.int32, (nb, bs), 0)
    klanes = [pl.ds((g * hd) // pair * pair, pair) for g in heads]
    kmean = jnp.sum(k_ref[...].astype(F32).reshape(nb, bs, hg * hd), axis=1) * (1.0 / bs)
    km_hi = kmean.astype(BF16)
    km_lo = (kmean - km_hi.astype(F32)).astype(BF16)
    km2 = []
    for g in heads:
        bias = slopes[g] * rel
        bias_ref[g, 0] = bias
        bias_ref[g, 1] = jnp.where(rel >= 0, bias, -NEG_INF)
        lo = (g * hd) // pair * pair
        km2.append(jnp.concatenate([km_hi[:, lo:lo + pair], km_lo[:, lo:lo + pair]], axis=0))

    def select(i, carry):
        past = blk_i < i
        own = jnp.where(blk_i == i, 1.0, 0.0)
        for g in heads:
            g2 = jnp.dot(km2[g], qT_ref[0, g, i], preferred_element_type=F32)
            gate = jnp.where(past, g2[:nb] + g2[nb:], NEG_INF)
            sel = jnp.zeros((nb, bs), F32)
            for _ in range(MOBA_TOPK):
                best = jnp.max(gate, axis=0, keepdims=True)
                first = jnp.min(jnp.where(gate == best, blk_i, nb), axis=0, keepdims=True)
                pick = blk_i == first
                sel = jnp.where(pick, 1.0, sel)
                gate = jnp.where(pick, PICKED, gate)
            sel_ref[g, i] = jnp.where(past, sel, own)
        return carry

    lax.fori_loop(0, nb, select, 0, unroll=2)

    ROW_M, ROW_A, ROW_ZMAX = 0, 1, 3
    ones_rows = jnp.ones((acc_ref.shape[1] - hd, bs), BF16)
    z_ref[...] = jnp.zeros_like(z_ref)
    p_ref[...] = jnp.zeros_like(p_ref)
    acc0 = jnp.where(lax.broadcasted_iota(jnp.int32, acc_ref.shape[1:], 0) < hd, 0.0, 1.0)
    st0 = jnp.where(lax.broadcasted_iota(jnp.int32, (8, bs), 0) == ROW_M, NEG_INF, 1.0)
    for g in heads:
        acc_ref[g] = acc0
        st_ref[g] = st0

    def step(s, par):
        sa = jnp.minimum(s, n_steps - 1)
        sb = jnp.clip(s - 1, 0, n_steps - 1)
        sc = jnp.clip(s - 2, 0, n_steps - 1)
        qa, ka = qblk_ref[sa], kblk_ref[sa]
        qb, kb = qblk_ref[sb], kblk_ref[sb]
        qc, kc = qblk_ref[sc], kblk_ref[sc]
        own_a = (qa == ka).astype(jnp.int32)
        own = qb == kb
        for g in heads:
            k_blk = k_ref[pl.ds(pl.multiple_of(ka * bs, bs), bs), klanes[g]]
            scores = jnp.dot(k_blk, qT_ref[0, g, qa], preferred_element_type=F32)
            z = scores - bias_ref[g, own_a]
            z_ref[par, g] = z
            st_ref[g, pl.ds(ROW_ZMAX + par, 1), :] = jnp.max(z, axis=0, keepdims=True)

            off = slopes[g] * lax.convert_element_type((qb - kb) * bs, F32)
            chosen = jnp.logical_and(sel_ref[g, qb, pl.ds(kb, 1), :] > 0.0, s <= n_steps)
            m_in = jnp.where(own, NEG_INF, st_ref[g, pl.ds(ROW_M, 1), :])
            tile_max = st_ref[g, pl.ds(ROW_ZMAX + 1 - par, 1), :]
            m_new = jnp.where(chosen, jnp.maximum(m_in, tile_max - off), m_in)
            p_f = jnp.exp2(z_ref[1 - par, g] - jnp.where(chosen, m_new + off, -NEG_INF))
            p_ref[1 - par, g] = p_f.astype(BF16)
            st_ref[g, pl.ds(ROW_M, 1), :] = m_new
            st_ref[g, pl.ds(ROW_A + 1 - par, 1), :] = jnp.exp2(m_in - m_new)

            a = st_ref[g, pl.ds(ROW_A + par, 1), :]
            v_aug = jnp.concatenate([vT_ref[0, g, kc], ones_rows], axis=0)
            acc_new = a * acc_ref[g] + jnp.dot(v_aug, p_ref[par, g], preferred_element_type=F32)
            acc_ref[g] = acc_new
            o_ref[0, g, qc] = (acc_new[:hd] / acc_new[hd:hd + 1]).astype(o_ref.dtype)

    def unrolled(t, carry):
        for u in range(MOBA_UNROLL):
            step(MOBA_UNROLL * t + u, u % 2)
        return carry

    lax.fori_loop(0, -(-(n_steps + 2) // MOBA_UNROLL), unrolled, 0)


def moba_attention(qT, k, vT, slopes, heads_per_step=4):
    b, h, nb, hd, bs = vT.shape
    hg = heads_per_step
    steps = [(i, i if t == 0 else t - 1) for i in range(nb) for t in range(i + 1)]
    qblk = jnp.asarray([s[0] for s in steps], jnp.int32)
    kblk = jnp.asarray([s[1] for s in steps], jnp.int32)
    blk = lambda s: pl.BlockSpec((1, hg) + s, lambda bi, hi, *_: (bi, hi, 0, 0, 0))
    return pl.pallas_call(
        _moba_kernel,
        out_shape=jax.ShapeDtypeStruct(vT.shape, BF16),
        grid_spec=pltpu.PrefetchScalarGridSpec(
            num_scalar_prefetch=3,
            grid=(b, h // hg),
            in_specs=[blk((nb, qT.shape[3], bs)),
                      pl.BlockSpec((nb * bs, hg * hd), lambda bi, hi, *_: (bi, hi)),
                      blk((nb, hd, bs))],
            out_specs=blk((nb, hd, bs)),
            scratch_shapes=[pltpu.VMEM((hg, nb, nb, bs), F32),
                            pltpu.VMEM((hg, 2, bs, bs), F32),
                            pltpu.VMEM((2, hg, bs, bs), F32),
                            pltpu.VMEM((2, hg, bs, bs), BF16),
                            pltpu.VMEM((hg, hd + 16, bs), F32),
                            pltpu.VMEM((hg, 8, bs), F32)]),
        compiler_params=_cparams("parallel", "parallel"),
        name="moba_attention",
    )(slopes, qblk, kblk, qT, k, vT)


def _swa_kernel(slope_ref, sink_ref, qT_ref, k_ref, vT_ref, o_ref, pbias_ref, cbias_ref, z_ref):
    nb = qT_ref.shape[2]
    w = vT_ref.shape[4]
    gw = qT_ref.shape[4]
    slope = slope_ref[0]
    sink = sink_ref[0]
    key_i = lax.broadcasted_iota(jnp.int32, (w, gw), 0)
    qry_i = lax.broadcasted_iota(jnp.int32, (w, gw), 1) % w
    d_prev = (w + qry_i - key_i).astype(F32)
    d_cur = (qry_i - key_i).astype(F32)
    pbias_ref[...] = jnp.where(d_prev < w, slope * d_prev, -NEG_INF)
    cbias_ref[...] = jnp.where(d_cur >= 0, slope * d_cur, -NEG_INF)

    def scores(n, slot):
        qT = qT_ref[0, 0, n]
        prev = jnp.maximum(n - 1, 0)
        no_prev = jnp.where(n == 0, -NEG_INF, 0.0)
        k_prev = k_ref[pl.ds(pl.multiple_of(prev * w, w), w), :]
        k_cur = k_ref[pl.ds(pl.multiple_of(n * w, w), w), :]
        z_ref[slot, 0] = jnp.dot(k_prev, qT, preferred_element_type=F32) - pbias_ref[...] - no_prev
        z_ref[slot, 1] = jnp.dot(k_cur, qT, preferred_element_type=F32) - cbias_ref[...]

    def attend(n, slot):
        zp, zc = z_ref[slot, 0], z_ref[slot, 1]
        prev = jnp.maximum(n - 1, 0)
        m = jnp.maximum(jnp.max(zp, axis=0, keepdims=True), jnp.max(zc, axis=0, keepdims=True))
        m = jnp.maximum(m, sink)
        pp = jnp.exp(zp - m)
        pc = jnp.exp(zc - m)
        l = jnp.sum(pp, axis=0, keepdims=True) + jnp.sum(pc, axis=0, keepdims=True) + jnp.exp(sink - m)
        acc = jnp.dot(vT_ref[0, 0, prev], pp.astype(BF16), preferred_element_type=F32)
        acc = acc + jnp.dot(vT_ref[0, 0, n], pc.astype(BF16), preferred_element_type=F32)
        o_ref[0, 0, n] = (acc / l).astype(o_ref.dtype)

    scores(0, 0)

    def two_blocks(t, carry):
        n = 2 * t
        scores(n + 1, 1)
        attend(n, 0)
        scores(jnp.minimum(n + 2, nb - 1), 0)
        attend(n + 1, 1)
        return carry

    lax.fori_loop(0, nb // 2, two_blocks, 0)


def swa_attention(qT, k, vT, slope_lanes, sink_lanes):
    b, kv, nb, hd, w = vT.shape
    gw = qT.shape[4]
    idx = lambda bi, ki: (bi, ki, 0, 0, 0)
    lane = pl.BlockSpec((1, 1, gw), lambda bi, ki: (ki, 0, 0))
    return pl.pallas_call(
        _swa_kernel,
        out_shape=jax.ShapeDtypeStruct((b, kv, nb, hd, gw), BF16),
        grid=(b, kv),
        in_specs=[lane, lane,
                  pl.BlockSpec((1, 1, nb, qT.shape[3], gw), idx),
                  pl.BlockSpec((nb * w, k.shape[1]), lambda bi, ki: (bi, 0)),
                  pl.BlockSpec((1, 1, nb, hd, w), idx)],
        out_specs=pl.BlockSpec((1, 1, nb, hd, gw), idx),
        scratch_shapes=[pltpu.VMEM((w, gw), F32), pltpu.VMEM((w, gw), F32), pltpu.VMEM((2, 2, w, gw), F32)],
        compiler_params=_cparams("parallel", "parallel"),
        name="swa_attention",
    )(slope_lanes, sink_lanes, qT, k, vT)


def _swiglu(xb, wg, wu, wd):
    hg = jnp.dot(xb, wg, preferred_element_type=F32)
    hu = jnp.dot(xb, wu, preferred_element_type=F32)
    hh = hg * jax.nn.sigmoid(hg) * hu
    return jnp.dot(hh.astype(BF16), wd, preferred_element_type=F32)


def _ffn_kernel(x_ref, wg_ref, wu_ref, wd_ref, g_ref, b_ref, o_ref):
    ffn = _swiglu(x_ref[...].astype(BF16), wg_ref[...], wu_ref[...], wd_ref[...])
    y = DEEPNORM_ALPHA * x_ref[...] + ffn
    o_ref[...] = _layer_norm(y, g_ref[...], b_ref[...])


def ffn_ln(x, wg, wu, wd, g, b, tm=512):
    m, d = x.shape
    f = wg.shape[1]
    tm = min(tm, m)
    whole = lambda s: pl.BlockSpec(s, lambda t: (0,) * len(s))
    return pl.pallas_call(
        _ffn_kernel,
        out_shape=jax.ShapeDtypeStruct((m, d), F32),
        grid=(m // tm,),
        in_specs=[pl.BlockSpec((tm, d), lambda t: (t, 0)),
                  whole((d, f)), whole((d, f)), whole((f, d)), whole((1, d)), whole((1, d))],
        out_specs=pl.BlockSpec((tm, d), lambda t: (t, 0)),
        compiler_params=_cparams("parallel"),
        name="ffn_ln",
    )(x, wg, wu, wd, g.reshape(1, d), b.reshape(1, d))


def _router_kernel(x_ref, wrT_ref, gate_ref, rank_ref, count_ref):
    tt = x_ref.shape[0]
    logits = lax.dot_general(wrT_ref[...], x_ref[...], (((1,), (1,)), ((), ())),
                             preferred_element_type=F32, precision=lax.Precision.HIGHEST)
    e_i = lax.broadcasted_iota(jnp.int32, logits.shape, 0)
    work = logits
    vals, picks = [], []
    for _ in range(MOE_TOPK):
        best = jnp.max(work, axis=0, keepdims=True)
        first = jnp.min(jnp.where(work == best, e_i, N_EXPERTS), axis=0, keepdims=True)
        pick = e_i == first
        vals.append(best)
        picks.append(pick)
        work = jnp.where(pick, PICKED, work)
    ex = [jnp.exp(v - vals[0]) for v in vals]
    denom = functools.reduce(lambda a, c: a + c, ex)
    gates = jnp.zeros(logits.shape, F32)
    sel = jnp.zeros(logits.shape, F32)
    for pick, e in zip(picks, ex):
        gates = jnp.where(pick, e / denom, gates)
        sel = jnp.where(pick, 1.0, sel)
    gate_ref[...] = gates
    before = (lax.broadcasted_iota(jnp.int32, (tt, tt), 0) < lax.broadcasted_iota(jnp.int32, (tt, tt), 1))
    rank = jnp.dot(sel.astype(BF16), jnp.where(before, 1.0, 0.0).astype(BF16), preferred_element_type=F32)
    rank_ref[...] = jnp.where(sel > 0.0, rank, -1.0)
    count_ref[0] = jnp.broadcast_to(jnp.sum(sel, axis=1, keepdims=True), count_ref.shape[1:])


def router(x, wrT, tt):
    n, d = x.shape
    ne = wrT.shape[0]
    return pl.pallas_call(
        _router_kernel,
        out_shape=(jax.ShapeDtypeStruct((ne, n), F32), jax.ShapeDtypeStruct((ne, n), F32),
                   jax.ShapeDtypeStruct((n // tt, ne, 128), F32)),
        grid=(n // tt,),
        in_specs=[pl.BlockSpec((tt, d), lambda t: (t, 0)), pl.BlockSpec((ne, d), lambda t: (0, 0))],
        out_specs=(pl.BlockSpec((ne, tt), lambda t: (0, t)), pl.BlockSpec((ne, tt), lambda t: (0, t)),
                   pl.BlockSpec((1, ne, 128), lambda t: (t, 0, 0))),
        compiler_params=_cparams("parallel"),
        name="router",
    )(x, wrT)


MOE_BIG = 512
MOE_SMALL = (128, 256, 320)
MOE_TOKEN_TILE = 1024
MOE_FF_SPLIT = 2


def _moe_kernel(count_ref, x_ref, gate_ref, rank_ref, wg_ref, wu_ref, wd_ref, g_ref, b_ref, o_ref,
                xb_ref, xe_ref, ye_ref):
    t, e, half = pl.program_id(0), pl.program_id(1), pl.program_id(2)
    ne, nh = pl.num_programs(1), pl.num_programs(2)
    tt = x_ref.shape[0]
    count = count_ref[t * ne + e]
    n_big = count // MOE_BIG
    rest = count - n_big * MOE_BIG
    rank_row = rank_ref[pl.ds(e, 1), :]

    def one_hot(r0, m):
        row_i = lax.broadcasted_iota(jnp.int32, (m, tt), 0).astype(F32)
        return rank_row == row_i + lax.convert_element_type(r0, F32)

    def row_groups(fn):
        def body(i, carry):
            fn(pl.multiple_of(i * MOE_BIG, MOE_BIG), MOE_BIG)
            return carry
        lax.fori_loop(0, n_big + (rest > MOE_SMALL[-1]).astype(jnp.int32), body, 0)
        lo = 0
        for m in MOE_SMALL:
            @pl.when(jnp.logical_and(rest > lo, rest <= m))
            def _():
                fn(pl.multiple_of(n_big * MOE_BIG, MOE_BIG), m)
            lo = m

    @pl.when(jnp.logical_and(e == 0, half == 0))
    def _():
        o_ref[...] = jnp.zeros_like(o_ref)
        xb_ref[...] = x_ref[...].astype(BF16)

    @pl.when(half == 0)
    def _():
        def gather(r0, m):
            hot = jnp.where(one_hot(r0, m), 1.0, 0.0).astype(BF16)
            xe_ref[pl.ds(r0, m), :] = jnp.dot(hot, xb_ref[...], preferred_element_type=F32).astype(BF16)
            ye_ref[pl.ds(r0, m), :] = jnp.zeros((m, ye_ref.shape[1]), F32)
        row_groups(gather)

    def expert(r0, m):
        ye_ref[pl.ds(r0, m), :] += _swiglu(xe_ref[pl.ds(r0, m), :], wg_ref[0], wu_ref[0], wd_ref[0])
    row_groups(expert)

    @pl.when(half == nh - 1)
    def _():
        gate_row = gate_ref[pl.ds(e, 1), :]

        def scatter(r0, m):
            hot = one_hot(r0, m)
            w_col = jnp.sum(jnp.where(hot, gate_row, 0.0), axis=1, keepdims=True)
            yw = (ye_ref[pl.ds(r0, m), :] * w_col).astype(BF16)
            o_ref[...] += lax.dot_general(jnp.where(hot, 1.0, 0.0).astype(BF16), yw,
                                          (((0,), (0,)), ((), ())), preferred_element_type=F32)
        row_groups(scatter)

    @pl.when(jnp.logical_and(e == ne - 1, half == nh - 1))
    def _():
        y = DEEPNORM_ALPHA * x_ref[...] + o_ref[...]
        o_ref[...] = _layer_norm(y, g_ref[...], b_ref[...])


def moe_ffn_ln(x, gate, rank, count, wg, wu, wd, g, b, tt):
    n, d = x.shape
    ne, fh = wg.shape[0], wg.shape[2] // MOE_FF_SPLIT
    w_cols = pl.BlockSpec((1, d, fh), lambda t, e, h, *_: (e, 0, h))
    w_rows = pl.BlockSpec((1, fh, d), lambda t, e, h, *_: (e, h, 0))
    row = pl.BlockSpec((1, d), lambda t, e, h, *_: (0, 0))
    return pl.pallas_call(
        _moe_kernel,
        out_shape=jax.ShapeDtypeStruct((n, d), F32),
        grid_spec=pltpu.PrefetchScalarGridSpec(
            num_scalar_prefetch=1,
            grid=(n // tt, ne, MOE_FF_SPLIT),
            in_specs=[pl.BlockSpec((tt, d), lambda t, e, h, *_: (t, 0)),
                      pl.BlockSpec((ne, tt), lambda t, e, h, *_: (0, t)),
                      pl.BlockSpec((ne, tt), lambda t, e, h, *_: (0, t)),
                      w_cols, w_cols, w_rows, row, row],
            out_specs=pl.BlockSpec((tt, d), lambda t, e, h, *_: (t, 0)),
            scratch_shapes=[pltpu.VMEM((tt, d), BF16), pltpu.VMEM((tt, d), BF16), pltpu.VMEM((tt, d), F32)]),
        compiler_params=_cparams("parallel", "arbitrary", "arbitrary"),
        name="moe_ffn_ln",
    )(count, x, gate, rank, wg, wu, wd, g.reshape(1, d), b.reshape(1, d))


def _alibi_slopes(n):
    return jnp.exp2(-8.0 * jnp.arange(1, n + 1, dtype=F32) / n)


def kernel(x, w_qkv_a, w_o_a, w_kv_shared, w_q_b, w_o_b, sinks_b, w_gate_d, w_up_d, w_down_d,
           w_router, w_gate_e, w_up_e, w_down_e, ln_gain, ln_bias):
    B, T, D = x.shape
    hd = HEAD_DIM
    H = D // hd
    N = B * T
    xs = x.reshape(N, D)
    q_scale = hd ** -0.5

    HA = w_qkv_a.shape[-1] // 3
    wq, wk, wv = (w_qkv_a[0][:, i * HA:(i + 1) * HA] for i in range(3))
    qT, k, vT = qkv_proj(xs, (wq * (q_scale * LOG2E)).T.astype(BF16), wk.astype(BF16), wv.T.astype(BF16),
                         batch=B, hq=H, hv=H, hd=hd, bs=MOBA_BLOCK, pad=2, q_groups=1)
    oT = moba_attention(qT, k, vT, _alibi_slopes(H))
    xs = oT_proj_ln(oT, w_o_a[0].astype(BF16), xs, ln_gain[0, 0], ln_bias[0, 0], MOBA_BLOCK)

    xs = ffn_ln(xs, w_gate_d[0].astype(BF16), w_up_d[0].astype(BF16), w_down_d[0].astype(BF16),
                ln_gain[0, 1], ln_bias[0, 1])

    KV = N_KV_B
    G = H // KV
    W = WINDOW
    wk, wv = w_kv_shared[:, :KV * hd], w_kv_shared[:, KV * hd:]
    qT, k, vT = qkv_proj(xs, (w_q_b[0] * q_scale).T.astype(BF16), wk.astype(BF16), wv.T.astype(BF16),
                         batch=B, hq=KV, hv=KV, hd=hd, bs=W, pad=KV, q_groups=G)
    slope_lanes = jnp.repeat(_alibi_slopes(H).reshape(KV, G), W, axis=1).reshape(KV, 1, G * W)
    sink_lanes = jnp.repeat(sinks_b[0].astype(F32).reshape(KV, G), W, axis=1).reshape(KV, 1, G * W)
    oT = swa_attention(qT, k, vT, slope_lanes, sink_lanes)
    xs = oT_proj_ln(oT, w_o_b[0].astype(BF16), xs, ln_gain[1, 0], ln_bias[1, 0], W)

    tt = min(MOE_TOKEN_TILE, N)
    gate, rank, count = router(xs, w_router[0].T, tt)
    count = count[:, :, 0].astype(jnp.int32).reshape(-1)
    xs = moe_ffn_ln(xs, gate, rank, count, w_gate_e[0].astype(BF16), w_up_e[0].astype(BF16),
                    w_down_e[0].astype(BF16), ln_gain[1, 1], ln_bias[1, 1], tt)
    return xs.reshape(B, T, D)
```

```python
import functools

import jax
import jax.numpy as jnp
from jax import lax
from jax.experimental import pallas as pl
from jax.experimental.pallas import tpu as pltpu

HEAD_DIM = 64
MOBA_BLOCK = 256
MOBA_TOPK = 3
MOBA_UNROLL = 4
PROJ_ROWS = 256
N_KV_B = 2
WINDOW = 128
N_EXPERTS = 8
MOE_TOPK = 2
DEPTH = 2
DEEPNORM_ALPHA = (2 * DEPTH) ** 0.25
LN_EPS = 1e-5
NEG_INF = -1e30
PICKED = -3e38
LOG2E = 1.4426950408889634
VMEM_LIMIT = 56 * 1024 * 1024

F32 = jnp.float32
BF16 = jnp.bfloat16


def _cparams(*sem):
    return pltpu.CompilerParams(dimension_semantics=sem, vmem_limit_bytes=VMEM_LIMIT)


def _layer_norm(y, g, b):
    mu = jnp.mean(y, axis=-1, keepdims=True)
    yc = y - mu
    var = jnp.mean(yc * yc, axis=-1, keepdims=True)
    return yc * lax.rsqrt(var + LN_EPS) * g + b


_NT = (((1,), (1,)), ((), ()))
_TN = (((0,), (0,)), ((), ()))


def _qkv_kernel(x_ref, wqT_ref, wk_ref, wvT_ref, qT_ref, k_ref, vT_ref, *, q_groups):
    xb = x_ref[...].astype(BF16)
    k_ref[...] = jnp.dot(xb, wk_ref[...], preferred_element_type=F32).astype(BF16)
    qT = lax.dot_general(wqT_ref[...], xb, _NT, preferred_element_type=F32).astype(BF16)
    vT = lax.dot_general(wvT_ref[...], xb, _NT, preferred_element_type=F32).astype(BF16)
    _, hv, nblk, hd, bs = vT_ref.shape
    _, hq, _, padded, ql = qT_ref.shape
    pad = padded // hd
    zero = jnp.zeros((hd, bs), BF16)
    for blk in range(nblk):
        cols = slice(blk * bs, (blk + 1) * bs)
        for h in range(hv):
            vT_ref[0, h, blk] = vT[h * hd:(h + 1) * hd, cols]
        for h in range(hq):
            lanes = []
            for g in range(q_groups):
                r0 = (h * q_groups + g) * hd
                parts = [zero] * pad
                parts[h % pad] = qT[r0:r0 + hd, cols]
                lanes.append(jnp.concatenate(parts, axis=0))
            qT_ref[0, h, blk] = lanes[0] if q_groups == 1 else jnp.concatenate(lanes, axis=1)


def qkv_proj(x, wqT, wk, wvT, batch, hq, hv, hd, bs, pad, q_groups, tm=1024):
    n, d = x.shape
    t = n // batch
    tm = min(tm, t)
    nblk = tm // bs
    steps = t // tm
    whole = lambda a: pl.BlockSpec(a.shape, lambda b, i: (0, 0))
    return pl.pallas_call(
        functools.partial(_qkv_kernel, q_groups=q_groups),
        out_shape=(jax.ShapeDtypeStruct((batch, hq, t // bs, pad * hd, q_groups * bs), BF16),
                   jax.ShapeDtypeStruct((n, wk.shape[1]), BF16),
                   jax.ShapeDtypeStruct((batch, hv, t // bs, hd, bs), BF16)),
        grid=(batch, steps),
        in_specs=[pl.BlockSpec((tm, d), lambda b, i: (b * steps + i, 0)), whole(wqT), whole(wk), whole(wvT)],
        out_specs=(pl.BlockSpec((1, hq, nblk, pad * hd, q_groups * bs), lambda b, i: (b, 0, i, 0, 0)),
                   pl.BlockSpec((tm, wk.shape[1]), lambda b, i: (b * steps + i, 0)),
                   pl.BlockSpec((1, hv, nblk, hd, bs), lambda b, i: (b, 0, i, 0, 0))),
        compiler_params=_cparams("parallel", "parallel"),
        name="qkv_proj",
    )(x, wqT, wk, wvT)


def _oT_proj_ln_kernel(oT_ref, w_ref, x_ref, g_ref, b_ref, o_ref, *, bs):
    _, ho, nblk, hd, lanes = oT_ref.shape
    per = max(1, PROJ_ROWS // bs)
    mixes = []
    for first in range(0, nblk, per):
        tiles = []
        for blk in range(first, first + per):
            heads = [oT_ref[0, h, blk, :, g * bs:(g + 1) * bs] for h in range(ho) for g in range(lanes // bs)]
            tiles.append(jnp.concatenate(heads, axis=0))
        aT = tiles[0] if per == 1 else jnp.concatenate(tiles, axis=1)
        mixes.append(lax.dot_general(aT, w_ref[...], _TN, preferred_element_type=F32))
    for i, mix in enumerate(mixes):
        rows = pl.ds(i * per * bs, per * bs)
        y = DEEPNORM_ALPHA * x_ref[rows, :] + mix
        o_ref[rows, :] = _layer_norm(y, g_ref[...], b_ref[...])


def oT_proj_ln(oT, w, x, g, b, bs, tm=1024):
    batch, ho, nb, hd, lanes = oT.shape
    n, d = x.shape
    t = n // batch
    tm = min(tm, t)
    nblk = tm // bs
    steps = t // tm
    row = pl.BlockSpec((1, d), lambda bi, i: (0, 0))
    return pl.pallas_call(
        functools.partial(_oT_proj_ln_kernel, bs=bs),
        out_shape=jax.ShapeDtypeStruct((n, d), F32),
        grid=(batch, steps),
        in_specs=[pl.BlockSpec((1, ho, nblk, hd, lanes), lambda bi, i: (bi, 0, i, 0, 0)),
                  pl.BlockSpec(w.shape, lambda bi, i: (0, 0)),
                  pl.BlockSpec((tm, d), lambda bi, i: (bi * steps + i, 0)), row, row],
        out_specs=pl.BlockSpec((tm, d), lambda bi, i: (bi * steps + i, 0)),
        compiler_params=_cparams("parallel", "parallel"),
        name="proj_ln",
    )(oT, w, x, g.reshape(1, d), b.reshape(1, d))


def _moba_kernel(slope_ref, qblk_ref, kblk_ref, qT_ref, k_ref, vT_ref, o_ref,
                 sel_ref, bias_ref, z_ref, p_ref, acc_ref, st_ref):
    _, hg, nb, hd, bs = vT_ref.shape
    pair = qT_ref.shape[3]
    n_steps = nb * (nb + 1) // 2
    heads = range(hg)
    slopes = [slope_ref[pl.program_id(1) * hg + g] * LOG2E for g in heads]

    key_i = lax.broadcasted_iota(jnp.int32, (bs, bs), 0)
    qry_i = lax.broadcasted_iota(jnp.int32, (bs, bs), 1)
    rel = (qry_i - key_i).astype(F32)
    blk_i = lax.broadcasted_iota(jnp.int32, (nb, bs), 0)
    klanes = [pl.ds((g * hd) // pair * pair, pair) for g in heads]
    kmean = jnp.sum(k_ref[...].astype(F32).reshape(nb, bs, hg * hd), axis=1) * (1.0 / bs)
    km_hi = kmean.astype(BF16)
    km_lo = (kmean - km_hi.astype(F32)).astype(BF16)
    km2 = []
    for g in heads:
        bias = slopes[g] * rel
        bias_ref[g, 0] = bias
        bias_ref[g, 1] = jnp.where(rel >= 0, bias, -NEG_INF)
        lo = (g * hd) // pair * pair
        km2.append(jnp.concatenate([km_hi[:, lo:lo + pair], km_lo[:, lo:lo + pair]], axis=0))

    def select(i, carry):
        past = blk_i < i
        own = jnp.where(blk_i == i, 1.0, 0.0)
        for g in heads:
            g2 = jnp.dot(km2[g], qT_ref[0, g, i], preferred_element_type=F32)
            gate = jnp.where(past, g2[:nb] + g2[nb:], NEG_INF)
            sel = jnp.zeros((nb, bs), F32)
            for _ in range(MOBA_TOPK):
                best = jnp.max(gate, axis=0, keepdims=True)
                first = jnp.min(jnp.where(gate == best, blk_i, nb), axis=0, keepdims=True)
                pick = blk_i == first
                sel = jnp.where(pick, 1.0, sel)
                gate = jnp.where(pick, PICKED, gate)
            sel_ref[g, i] = jnp.where(past, sel, own)
        return carry

    lax.fori_loop(0, nb, select, 0, unroll=2)

    ROW_M, ROW_A, ROW_ZMAX = 0, 1, 3
    ones_rows = jnp.ones((acc_ref.shape[1] - hd, bs), BF16)
    z_ref[...] = jnp.zeros_like(z_ref)
    p_ref[...] = jnp.zeros_like(p_ref)
    acc0 = jnp.where(lax.broadcasted_iota(jnp.int32, acc_ref.shape[1:], 0) < hd, 0.0, 1.0)
    st0 = jnp.where(lax.broadcasted_iota(jnp.int32, (8, bs), 0) == ROW_M, NEG_INF, 1.0)
    for g in heads:
        acc_ref[g] = acc0
        st_ref[g] = st0

    def step(s, par):
        sa = jnp.minimum(s, n_steps - 1)
        sb = jnp.clip(s - 1, 0, n_steps - 1)
        sc = jnp.clip(s - 2, 0, n_steps - 1)
        qa, ka = qblk_ref[sa], kblk_ref[sa]
        qb, kb = qblk_ref[sb], kblk_ref[sb]
        qc, kc = qblk_ref[sc], kblk_ref[sc]
        own_a = (qa == ka).astype(jnp.int32)
        own = qb == kb
        for g in heads:
            k_blk = k_ref[pl.ds(pl.multiple_of(ka * bs, bs), bs), klanes[g]]
            scores = jnp.dot(k_blk, qT_ref[0, g, qa], preferred_element_type=F32)
            z = scores - bias_ref[g, own_a]
            z_ref[par, g] = z
            st_ref[g, pl.ds(ROW_ZMAX + par, 1), :] = jnp.max(z, axis=0, keepdims=True)

            off = slopes[g] * lax.convert_element_type((qb - kb) * bs, F32)
            chosen = jnp.logical_and(sel_ref[g, qb, pl.ds(kb, 1), :] > 0.0, s <= n_steps)
            m_in = jnp.where(own, NEG_INF, st_ref[g, pl.ds(ROW_M, 1), :])
            tile_max = st_ref[g, pl.ds(ROW_ZMAX + 1 - par, 1), :]
            m_new = jnp.where(chosen, jnp.maximum(m_in, tile_max - off), m_in)
            p_f = jnp.exp2(z_ref[1 - par, g] - jnp.where(chosen, m_new + off, -NEG_INF))
            p_ref[1 - par, g] = p_f.astype(BF16)
            st_ref[g, pl.ds(ROW_M, 1), :] = m_new
            st_ref[g, pl.ds(ROW_A + 1 - par, 1), :] = jnp.exp2(m_in - m_new)

            a = st_ref[g, pl.ds(ROW_A + par, 1), :]
            v_aug = jnp.concatenate([vT_ref[0, g, kc], ones_rows], axis=0)
            acc_new = a * acc_ref[g] + jnp.dot(v_aug, p_ref[par, g], preferred_element_type=F32)
            acc_ref[g] = acc_new
            o_ref[0, g, qc] = (acc_new[:hd] / acc_new[hd:hd + 1]).astype(o_ref.dtype)

    def unrolled(t, carry):
        for u in range(MOBA_UNROLL):
            step(MOBA_UNROLL * t + u, u % 2)
        return carry

    lax.fori_loop(0, -(-(n_steps + 2) // MOBA_UNROLL), unrolled, 0)


def moba_attention(qT, k, vT, slopes, heads_per_step=4):
    b, h, nb, hd, bs = vT.shape
    hg = heads_per_step
    steps = [(i, i if t == 0 else t - 1) for i in range(nb) for t in range(i + 1)]
    qblk = jnp.asarray([s[0] for s in steps], jnp.int32)
    kblk = jnp.asarray([s[1] for s in steps], jnp.int32)
    blk = lambda s: pl.BlockSpec((1, hg) + s, lambda bi, hi, *_: (bi, hi, 0, 0, 0))
    return pl.pallas_call(
        _moba_kernel,
        out_shape=jax.ShapeDtypeStruct(vT.shape, BF16),
        grid_spec=pltpu.PrefetchScalarGridSpec(
            num_scalar_prefetch=3,
            grid=(b, h // hg),
            in_specs=[blk((nb, qT.shape[3], bs)),
                      pl.BlockSpec((nb * bs, hg * hd), lambda bi, hi, *_: (bi, hi)),
                      blk((nb, hd, bs))],
            out_specs=blk((nb, hd, bs)),
            scratch_shapes=[pltpu.VMEM((hg, nb, nb, bs), F32),
                            pltpu.VMEM((hg, 2, bs, bs), F32),
                            pltpu.VMEM((2, hg, bs, bs), F32),
                            pltpu.VMEM((2, hg, bs, bs), BF16),
                            pltpu.VMEM((hg, hd + 16, bs), F32),
                            pltpu.VMEM((hg, 8, bs), F32)]),
        compiler_params=_cparams("parallel", "parallel"),
        name="moba_attention",
    )(slopes, qblk, kblk, qT, k, vT)


def _swa_kernel(slope_ref, sink_ref, qT_ref, k_ref, vT_ref, o_ref, pbias_ref, cbias_ref, z_ref):
    nb = qT_ref.shape[2]
    w = vT_ref.shape[4]
    gw = qT_ref.shape[4]
    slope = slope_ref[0]
    sink = sink_ref[0]
    key_i = lax.broadcasted_iota(jnp.int32, (w, gw), 0)
    qry_i = lax.broadcasted_iota(jnp.int32, (w, gw), 1) % w
    d_prev = (w + qry_i - key_i).astype(F32)
    d_cur = (qry_i - key_i).astype(F32)
    pbias_ref[...] = jnp.where(d_prev < w, slope * d_prev, -NEG_INF)
    cbias_ref[...] = jnp.where(d_cur >= 0, slope * d_cur, -NEG_INF)

    def scores(n, slot):
        qT = qT_ref[0, 0, n]
        prev = jnp.maximum(n - 1, 0)
        no_prev = jnp.where(n == 0, -NEG_INF, 0.0)
        k_prev = k_ref[pl.ds(pl.multiple_of(prev * w, w), w), :]
        k_cur = k_ref[pl.ds(pl.multiple_of(n * w, w), w), :]
        z_ref[slot, 0] = jnp.dot(k_prev, qT, preferred_element_type=F32) - pbias_ref[...] - no_prev
        z_ref[slot, 1] = jnp.dot(k_cur, qT, preferred_element_type=F32) - cbias_ref[...]

    hd = vT_ref.shape[3]
    ones_rows = jnp.ones((16, w), BF16)

    def attend(n, slot):
        zp, zc = z_ref[slot, 0], z_ref[slot, 1]
        prev = jnp.maximum(n - 1, 0)
        m = jnp.maximum(jnp.max(zp, axis=0, keepdims=True), jnp.max(zc, axis=0, keepdims=True))
        m = jnp.maximum(m, sink)
        pp = jnp.exp2(zp - m).astype(BF16)
        pc = jnp.exp2(zc - m).astype(BF16)
        v_prev = jnp.concatenate([vT_ref[0, 0, prev], ones_rows], axis=0)
        v_cur = jnp.concatenate([vT_ref[0, 0, n], ones_rows], axis=0)
        acc = jnp.dot(v_prev, pp, preferred_element_type=F32) + jnp.dot(v_cur, pc, preferred_element_type=F32)
        l = acc[hd:hd + 1] + jnp.exp2(sink - m)
        o_ref[0, 0, n] = (acc[:hd] / l).astype(o_ref.dtype)

    scores(0, 0)

    def two_blocks(t, carry):
        n = 2 * t
        scores(n + 1, 1)
        attend(n, 0)
        scores(jnp.minimum(n + 2, nb - 1), 0)
        attend(n + 1, 1)
        return carry

    lax.fori_loop(0, nb // 2, two_blocks, 0)


def swa_attention(qT, k, vT, slope_lanes, sink_lanes):
    b, kv, nb, hd, w = vT.shape
    gw = qT.shape[4]
    idx = lambda bi, ki: (bi, ki, 0, 0, 0)
    lane = pl.BlockSpec((1, 1, gw), lambda bi, ki: (ki, 0, 0))
    return pl.pallas_call(
        _swa_kernel,
        out_shape=jax.ShapeDtypeStruct((b, kv, nb, hd, gw), BF16),
        grid=(b, kv),
        in_specs=[lane, lane,
                  pl.BlockSpec((1, 1, nb, qT.shape[3], gw), idx),
                  pl.BlockSpec((nb * w, k.shape[1]), lambda bi, ki: (bi, 0)),
                  pl.BlockSpec((1, 1, nb, hd, w), idx)],
        out_specs=pl.BlockSpec((1, 1, nb, hd, gw), idx),
        scratch_shapes=[pltpu.VMEM((w, gw), F32), pltpu.VMEM((w, gw), F32), pltpu.VMEM((2, 2, w, gw), F32)],
        compiler_params=_cparams("parallel", "parallel"),
        name="swa_attention",
    )(slope_lanes, sink_lanes, qT, k, vT)


def _swiglu(xb, wg, wu, wd):
    hg = jnp.dot(xb, wg, preferred_element_type=F32)
    hu = jnp.dot(xb, wu, preferred_element_type=F32)
    hh = hg * jax.nn.sigmoid(hg) * hu
    return jnp.dot(hh.astype(BF16), wd, preferred_element_type=F32)


def _ffn_kernel(x_ref, wg_ref, wu_ref, wd_ref, g_ref, b_ref, o_ref):
    ffn = _swiglu(x_ref[...].astype(BF16), wg_ref[...], wu_ref[...], wd_ref[...])
    y = DEEPNORM_ALPHA * x_ref[...] + ffn
    o_ref[...] = _layer_norm(y, g_ref[...], b_ref[...])


def ffn_ln(x, wg, wu, wd, g, b, tm=512):
    m, d = x.shape
    f = wg.shape[1]
    tm = min(tm, m)
    whole = lambda s: pl.BlockSpec(s, lambda t: (0,) * len(s))
    return pl.pallas_call(
        _ffn_kernel,
        out_shape=jax.ShapeDtypeStruct((m, d), F32),
        grid=(m // tm,),
        in_specs=[pl.BlockSpec((tm, d), lambda t: (t, 0)),
                  whole((d, f)), whole((d, f)), whole((f, d)), whole((1, d)), whole((1, d))],
        out_specs=pl.BlockSpec((tm, d), lambda t: (t, 0)),
        compiler_params=_cparams("parallel"),
        name="ffn_ln",
    )(x, wg, wu, wd, g.reshape(1, d), b.reshape(1, d))


def _router_kernel(x_ref, wrT_ref, gate_ref, rank_ref, count_ref):
    tt = x_ref.shape[0]
    logits = lax.dot_general(wrT_ref[...], x_ref[...], (((1,), (1,)), ((), ())),
                             preferred_element_type=F32, precision=lax.Precision.HIGHEST)
    e_i = lax.broadcasted_iota(jnp.int32, logits.shape, 0)
    work = logits
    vals, picks = [], []
    for _ in range(MOE_TOPK):
        best = jnp.max(work, axis=0, keepdims=True)
        first = jnp.min(jnp.where(work == best, e_i, N_EXPERTS), axis=0, keepdims=True)
        pick = e_i == first
        vals.append(best)
        picks.append(pick)
        work = jnp.where(pick, PICKED, work)
    ex = [jnp.exp(v - vals[0]) for v in vals]
    denom = functools.reduce(lambda a, c: a + c, ex)
    gates = jnp.zeros(logits.shape, F32)
    sel = jnp.zeros(logits.shape, F32)
    for pick, e in zip(picks, ex):
        gates = jnp.where(pick, e / denom, gates)
        sel = jnp.where(pick, 1.0, sel)
    gate_ref[...] = gates
    before = (lax.broadcasted_iota(jnp.int32, (tt, tt), 0) < lax.broadcasted_iota(jnp.int32, (tt, tt), 1))
    rank = jnp.dot(sel.astype(BF16), jnp.where(before, 1.0, 0.0).astype(BF16), preferred_element_type=F32)
    rank_ref[...] = jnp.where(sel > 0.0, rank, -1.0)
    count_ref[0] = jnp.broadcast_to(jnp.sum(sel, axis=1, keepdims=True), count_ref.shape[1:])


def router(x, wrT, tt):
    n, d = x.shape
    ne = wrT.shape[0]
    return pl.pallas_call(
        _router_kernel,
        out_shape=(jax.ShapeDtypeStruct((ne, n), F32), jax.ShapeDtypeStruct((ne, n), F32),
                   jax.ShapeDtypeStruct((n // tt, ne, 128), F32)),
        grid=(n // tt,),
        in_specs=[pl.BlockSpec((tt, d), lambda t: (t, 0)), pl.BlockSpec((ne, d), lambda t: (0, 0))],
        out_specs=(pl.BlockSpec((ne, tt), lambda t: (0, t)), pl.BlockSpec((ne, tt), lambda t: (0, t)),
                   pl.BlockSpec((1, ne, 128), lambda t: (t, 0, 0))),
        compiler_params=_cparams("parallel"),
        name="router",
    )(x, wrT)


MOE_BIG = 512
MOE_SMALL = (256, 288, 320)
MOE_TOKEN_TILE = 1024
MOE_FF_SPLIT = 2


def _moe_kernel(count_ref, x_ref, gate_ref, rank_ref, wg_ref, wu_ref, wd_ref, g_ref, b_ref, o_ref,
                xb_ref, xe_ref, ye_ref):
    t, e, half = pl.program_id(0), pl.program_id(1), pl.program_id(2)
    ne, nh = pl.num_programs(1), pl.num_programs(2)
    tt = x_ref.shape[0]
    count = count_ref[t * ne + e]
    n_big = count // MOE_BIG
    rest = count - n_big * MOE_BIG
    rank_row = rank_ref[pl.ds(e, 1), :]

    def one_hot(r0, m):
        row_i = lax.broadcasted_iota(jnp.int32, (m, tt), 0).astype(F32)
        return rank_row == row_i + lax.convert_element_type(r0, F32)

    def row_groups(fn, sizes=MOE_SMALL[-1:]):
        def body(i, carry):
            fn(pl.multiple_of(i * MOE_BIG, MOE_BIG), MOE_BIG)
            return carry
        lax.fori_loop(0, n_big + (rest > MOE_SMALL[-1]).astype(jnp.int32), body, 0)
        lo = 0
        for m in sizes:
            @pl.when(jnp.logical_and(rest > lo, rest <= m))
            def _():
                fn(pl.multiple_of(n_big * MOE_BIG, MOE_BIG), m)
            lo = m

    @pl.when(jnp.logical_and(e == 0, half == 0))
    def _():
        o_ref[...] = jnp.zeros_like(o_ref)
        xb_ref[...] = x_ref[...].astype(BF16)

    @pl.when(half == 0)
    def _():
        def gather(r0, m):
            hot = jnp.where(one_hot(r0, m), 1.0, 0.0).astype(BF16)
            xe_ref[pl.ds(r0, m), :] = jnp.dot(hot, xb_ref[...], preferred_element_type=F32).astype(BF16)
            ye_ref[pl.ds(r0, m), :] = jnp.zeros((m, ye_ref.shape[1]), F32)
        row_groups(gather)

    def expert(r0, m):
        ye_ref[pl.ds(r0, m), :] += _swiglu(xe_ref[pl.ds(r0, m), :], wg_ref[0], wu_ref[0], wd_ref[0])
    row_groups(expert, MOE_SMALL)

    @pl.when(half == nh - 1)
    def _():
        gate_row = gate_ref[pl.ds(e, 1), :]

        def scatter(r0, m):
            hot = one_hot(r0, m)
            w_col = jnp.sum(jnp.where(hot, gate_row, 0.0), axis=1, keepdims=True)
            yw = (ye_ref[pl.ds(r0, m), :] * w_col).astype(BF16)
            o_ref[...] += lax.dot_general(jnp.where(hot, 1.0, 0.0).astype(BF16), yw,
                                          (((0,), (0,)), ((), ())), preferred_element_type=F32)
        row_groups(scatter)

    @pl.when(jnp.logical_and(e == ne - 1, half == nh - 1))
    def _():
        y = DEEPNORM_ALPHA * x_ref[...] + o_ref[...]
        o_ref[...] = _layer_norm(y, g_ref[...], b_ref[...])


def moe_ffn_ln(x, gate, rank, count, wg, wu, wd, g, b, tt):
    n, d = x.shape
    ne, fh = wg.shape[0], wg.shape[2] // MOE_FF_SPLIT
    w_cols = pl.BlockSpec((1, d, fh), lambda t, e, h, *_: (e, 0, h))
    w_rows = pl.BlockSpec((1, fh, d), lambda t, e, h, *_: (e, h, 0))
    row = pl.BlockSpec((1, d), lambda t, e, h, *_: (0, 0))
    return pl.pallas_call(
        _moe_kernel,
        out_shape=jax.ShapeDtypeStruct((n, d), F32),
        grid_spec=pltpu.PrefetchScalarGridSpec(
            num_scalar_prefetch=1,
            grid=(n // tt, ne, MOE_FF_SPLIT),
            in_specs=[pl.BlockSpec((tt, d), lambda t, e, h, *_: (t, 0)),
                      pl.BlockSpec((ne, tt), lambda t, e, h, *_: (0, t)),
                      pl.BlockSpec((ne, tt), lambda t, e, h, *_: (0, t)),
                      w_cols, w_cols, w_rows, row, row],
            out_specs=pl.BlockSpec((tt, d), lambda t, e, h, *_: (t, 0)),
            scratch_shapes=[pltpu.VMEM((tt, d), BF16), pltpu.VMEM((tt, d), BF16), pltpu.VMEM((tt, d), F32)]),
        compiler_params=_cparams("parallel", "arbitrary", "arbitrary"),
        name="moe_ffn_ln",
    )(count, x, gate, rank, wg, wu, wd, g.reshape(1, d), b.reshape(1, d))


def _alibi_slopes(n):
    return jnp.exp2(-8.0 * jnp.arange(1, n + 1, dtype=F32) / n)


def kernel(x, w_qkv_a, w_o_a, w_kv_shared, w_q_b, w_o_b, sinks_b, w_gate_d, w_up_d, w_down_d,
           w_router, w_gate_e, w_up_e, w_down_e, ln_gain, ln_bias):
    B, T, D = x.shape
    hd = HEAD_DIM
    H = D // hd
    N = B * T
    xs = x.reshape(N, D)
    q_scale = hd ** -0.5

    HA = w_qkv_a.shape[-1] // 3
    wq, wk, wv = (w_qkv_a[0][:, i * HA:(i + 1) * HA] for i in range(3))
    qT, k, vT = qkv_proj(xs, (wq * (q_scale * LOG2E)).T.astype(BF16), wk.astype(BF16), wv.T.astype(BF16),
                         batch=B, hq=H, hv=H, hd=hd, bs=MOBA_BLOCK, pad=2, q_groups=1)
    oT = moba_attention(qT, k, vT, _alibi_slopes(H))
    xs = oT_proj_ln(oT, w_o_a[0].astype(BF16), xs, ln_gain[0, 0], ln_bias[0, 0], MOBA_BLOCK)

    xs = ffn_ln(xs, w_gate_d[0].astype(BF16), w_up_d[0].astype(BF16), w_down_d[0].astype(BF16),
                ln_gain[0, 1], ln_bias[0, 1])

    KV = N_KV_B
    G = H // KV
    W = WINDOW
    wk, wv = w_kv_shared[:, :KV * hd], w_kv_shared[:, KV * hd:]
    qT, k, vT = qkv_proj(xs, (w_q_b[0] * (q_scale * LOG2E)).T.astype(BF16), wk.astype(BF16), wv.T.astype(BF16),
                         batch=B, hq=KV, hv=KV, hd=hd, bs=W, pad=KV, q_groups=G)
    slope_lanes = jnp.repeat(_alibi_slopes(H).reshape(KV, G) * LOG2E, W, axis=1).reshape(KV, 1, G * W)
    sink_lanes = jnp.repeat(sinks_b[0].astype(F32).reshape(KV, G) * LOG2E, W, axis=1).reshape(KV, 1, G * W)
    oT = swa_attention(qT, k, vT, slope_lanes, sink_lanes)
    xs = oT_proj_ln(oT, w_o_b[0].astype(BF16), xs, ln_gain[1, 0], ln_bias[1, 0], W)

    tt = min(MOE_TOKEN_TILE, N)
    gate, rank, count = router(xs, w_router[0].T, tt)
    count = count[:, :, 0].astype(jnp.int32).reshape(-1)
    xs = moe_ffn_ln(xs, gate, rank, count, w_gate_e[0].astype(BF16), w_up_e[0].astype(BF16),
                    w_down_e[0].astype(BF16), ln_gain[1, 1], ln_bias[1, 1], tt)
    return xs.reshape(B, T, D)
```

```python
import functools

import jax
import jax.numpy as jnp
from jax import lax
from jax.experimental import pallas as pl
from jax.experimental.pallas import tpu as pltpu

HEAD_DIM = 64
MOBA_BLOCK = 256
MOBA_TOPK = 3
MOBA_UNROLL = 4
PROJ_ROWS = 256
N_KV_B = 2
WINDOW = 128
N_EXPERTS = 8
MOE_TOPK = 2
DEPTH = 2
DEEPNORM_ALPHA = (2 * DEPTH) ** 0.25
LN_EPS = 1e-5
NEG_INF = -1e30
PICKED = -3e38
LOG2E = 1.4426950408889634
VMEM_LIMIT = 56 * 1024 * 1024

F32 = jnp.float32
BF16 = jnp.bfloat16


def _cparams(*sem):
    return pltpu.CompilerParams(dimension_semantics=sem, vmem_limit_bytes=VMEM_LIMIT)


def _layer_norm(y, g, b):
    mu = jnp.mean(y, axis=-1, keepdims=True)
    yc = y - mu
    var = jnp.mean(yc * yc, axis=-1, keepdims=True)
    return yc * lax.rsqrt(var + LN_EPS) * g + b


_NT = (((1,), (1,)), ((), ()))
_TN = (((0,), (0,)), ((), ()))


def _qkv_kernel(x_ref, wqT_ref, wk_ref, wvT_ref, qT_ref, k_ref, vT_ref, *, q_groups):
    xb = x_ref[...].astype(BF16)
    k_ref[...] = jnp.dot(xb, wk_ref[...], preferred_element_type=F32).astype(BF16)
    qT = lax.dot_general(wqT_ref[...], xb, _NT, preferred_element_type=F32).astype(BF16)
    vT = lax.dot_general(wvT_ref[...], xb, _NT, preferred_element_type=F32).astype(BF16)
    _, hv, nblk, hd, bs = vT_ref.shape
    _, hq, _, padded, ql = qT_ref.shape
    pad = padded // hd
    zero = jnp.zeros((hd, bs), BF16)
    for blk in range(nblk):
        cols = slice(blk * bs, (blk + 1) * bs)
        for h in range(hv):
            vT_ref[0, h, blk] = vT[h * hd:(h + 1) * hd, cols]
        for h in range(hq):
            lanes = []
            for g in range(q_groups):
                r0 = (h * q_groups + g) * hd
                parts = [zero] * pad
                parts[h % pad] = qT[r0:r0 + hd, cols]
                lanes.append(jnp.concatenate(parts, axis=0))
            qT_ref[0, h, blk] = lanes[0] if q_groups == 1 else jnp.concatenate(lanes, axis=1)


def qkv_proj(x, wqT, wk, wvT, batch, hq, hv, hd, bs, pad, q_groups, tm=1024):
    n, d = x.shape
    t = n // batch
    tm = min(tm, t)
    nblk = tm // bs
    steps = t // tm
    whole = lambda a: pl.BlockSpec(a.shape, lambda b, i: (0, 0))
    return pl.pallas_call(
        functools.partial(_qkv_kernel, q_groups=q_groups),
        out_shape=(jax.ShapeDtypeStruct((batch, hq, t // bs, pad * hd, q_groups * bs), BF16),
                   jax.ShapeDtypeStruct((n, wk.shape[1]), BF16),
                   jax.ShapeDtypeStruct((batch, hv, t // bs, hd, bs), BF16)),
        grid=(batch, steps),
        in_specs=[pl.BlockSpec((tm, d), lambda b, i: (b * steps + i, 0)), whole(wqT), whole(wk), whole(wvT)],
        out_specs=(pl.BlockSpec((1, hq, nblk, pad * hd, q_groups * bs), lambda b, i: (b, 0, i, 0, 0)),
                   pl.BlockSpec((tm, wk.shape[1]), lambda b, i: (b * steps + i, 0)),
                   pl.BlockSpec((1, hv, nblk, hd, bs), lambda b, i: (b, 0, i, 0, 0))),
        compiler_params=_cparams("parallel", "parallel"),
        name="qkv_proj",
    )(x, wqT, wk, wvT)


def _oT_proj_ln_kernel(oT_ref, w_ref, x_ref, g_ref, b_ref, *rest, bs, then):
    if then == "ffn":
        wg_ref, wu_ref, wd_ref, g2_ref, b2_ref, o_ref = rest
    elif then == "route":
        wrT_ref, o_ref, gate_ref, rank_ref, count_ref = rest
    else:
        o_ref, = rest
    _, ho, nblk, hd, lanes = oT_ref.shape
    per = max(1, PROJ_ROWS // bs)
    mixes = []
    for first in range(0, nblk, per):
        tiles = []
        for blk in range(first, first + per):
            heads = [oT_ref[0, h, blk, :, g * bs:(g + 1) * bs] for h in range(ho) for g in range(lanes // bs)]
            tiles.append(jnp.concatenate(heads, axis=0))
        aT = tiles[0] if per == 1 else jnp.concatenate(tiles, axis=1)
        mixes.append(lax.dot_general(aT, w_ref[...], _TN, preferred_element_type=F32))
    for i, mix in enumerate(mixes):
        rows = pl.ds(i * per * bs, per * bs)
        y = DEEPNORM_ALPHA * x_ref[rows, :] + mix
        o_ref[rows, :] = _layer_norm(y, g_ref[...], b_ref[...])
    if then == "ffn":
        x1 = o_ref[...]
        ffn = _swiglu(x1.astype(BF16), wg_ref[...], wu_ref[...], wd_ref[...])
        o_ref[...] = _layer_norm(DEEPNORM_ALPHA * x1 + ffn, g2_ref[...], b2_ref[...])
    elif then == "route":
        gate_ref[...], rank_ref[...], count = _route(o_ref[...], wrT_ref[...])
        count_ref[0] = jnp.broadcast_to(count, count_ref.shape[1:])


def oT_proj_ln(oT, w, x, g, b, bs, tm, ffn=None, router_wT=None):
    batch, ho, nb, hd, lanes = oT.shape
    n, d = x.shape
    t = n // batch
    tm = min(tm, t)
    nblk = tm // bs
    steps = t // tm
    tile = pl.BlockSpec((tm, d), lambda bi, i: (bi * steps + i, 0))
    const = lambda a: pl.BlockSpec(a.shape, lambda bi, i: (0,) * a.ndim, pipeline_mode=pl.Buffered(1))
    args = [oT, w, x, g.reshape(1, d), b.reshape(1, d)]
    in_specs = [pl.BlockSpec((1, ho, nblk, hd, lanes), lambda bi, i: (bi, 0, i, 0, 0)), const(w), tile,
                const(args[3]), const(args[4])]
    out_shape, out_specs, then = jax.ShapeDtypeStruct((n, d), F32), tile, None
    if ffn is not None:
        wg, wu, wd, g2, b2 = ffn
        extra = [wg, wu, wd, g2.reshape(1, d), b2.reshape(1, d)]
        then = "ffn"
    elif router_wT is not None:
        ne = router_wT.shape[0]
        extra = [router_wT]
        then = "route"
        out_shape = (out_shape, jax.ShapeDtypeStruct((ne, n), F32), jax.ShapeDtypeStruct((ne, n), F32),
                     jax.ShapeDtypeStruct((n // tm, ne, 128), F32))
        lane_tile = pl.BlockSpec((ne, tm), lambda bi, i: (0, bi * steps + i))
        out_specs = (tile, lane_tile, lane_tile, pl.BlockSpec((1, ne, 128), lambda bi, i: (bi * steps + i, 0, 0)))
    else:
        extra = []
    return pl.pallas_call(
        functools.partial(_oT_proj_ln_kernel, bs=bs, then=then),
        out_shape=out_shape,
        grid=(batch, steps),
        in_specs=in_specs + [const(a) for a in extra],
        out_specs=out_specs,
        compiler_params=_cparams("parallel", "parallel"),
        name="proj_ln" + ("_" + then if then else ""),
    )(*args, *extra)


def _moba_kernel(slope_ref, qblk_ref, kblk_ref, qT_ref, k_ref, vT_ref, o_ref,
                 sel_ref, bias_ref, z_ref, p_ref, acc_ref, st_ref):
    _, hg, nb, hd, bs = vT_ref.shape
    pair = qT_ref.shape[3]
    n_steps = nb * (nb + 1) // 2
    heads = range(hg)
    slopes = [slope_ref[pl.program_id(1) * hg + g] * LOG2E for g in heads]

    key_i = lax.broadcasted_iota(jnp.int32, (bs, bs), 0)
    qry_i = lax.broadcasted_iota(jnp.int32, (bs, bs), 1)
    rel = (qry_i - key_i).astype(F32)
    blk_i = lax.broadcasted_iota(jnp.int32, (nb, bs), 0)
    klanes = [pl.ds((g * hd) // pair * pair, pair) for g in heads]
    kmean = jnp.sum(k_ref[...].astype(F32).reshape(nb, bs, hg * hd), axis=1) * (1.0 / bs)
    km_hi = kmean.astype(BF16)
    km_lo = (kmean - km_hi.astype(F32)).astype(BF16)
    km2 = []
    for g in heads:
        bias = slopes[g] * rel
        bias_ref[g, 0] = bias
        bias_ref[g, 1] = jnp.where(rel >= 0, bias, -NEG_INF)
        lo = (g * hd) // pair * pair
        km2.append(jnp.concatenate([km_hi[:, lo:lo + pair], km_lo[:, lo:lo + pair]], axis=0))

    def select(i, carry):
        past = blk_i < i
        own = jnp.where(blk_i == i, 1.0, 0.0)
        for g in heads:
            g2 = jnp.dot(km2[g], qT_ref[0, g, i], preferred_element_type=F32)
            gate = jnp.where(past, g2[:nb] + g2[nb:], NEG_INF)
            sel = jnp.zeros((nb, bs), F32)
            for _ in range(MOBA_TOPK):
                best = jnp.max(gate, axis=0, keepdims=True)
                first = jnp.min(jnp.where(gate == best, blk_i, nb), axis=0, keepdims=True)
                pick = blk_i == first
                sel = jnp.where(pick, 1.0, sel)
                gate = jnp.where(pick, PICKED, gate)
            sel_ref[g, i] = jnp.where(past, sel, own)
        return carry

    lax.fori_loop(0, nb, select, 0, unroll=2)

    ROW_M, ROW_A, ROW_ZMAX = 0, 1, 3
    ones_rows = jnp.ones((acc_ref.shape[1] - hd, bs), BF16)
    z_ref[...] = jnp.zeros_like(z_ref)
    p_ref[...] = jnp.zeros_like(p_ref)
    acc0 = jnp.where(lax.broadcasted_iota(jnp.int32, acc_ref.shape[1:], 0) < hd, 0.0, 1.0)
    st0 = jnp.where(lax.broadcasted_iota(jnp.int32, (8, bs), 0) == ROW_M, NEG_INF, 1.0)
    for g in heads:
        acc_ref[g] = acc0
        st_ref[g] = st0

    def step(s, par):
        sa = jnp.minimum(s, n_steps - 1)
        sb = jnp.clip(s - 1, 0, n_steps - 1)
        sc = jnp.clip(s - 2, 0, n_steps - 1)
        qa, ka = qblk_ref[sa], kblk_ref[sa]
        qb, kb = qblk_ref[sb], kblk_ref[sb]
        qc, kc = qblk_ref[sc], kblk_ref[sc]
        own_a = (qa == ka).astype(jnp.int32)
        own = qb == kb
        for g in heads:
            k_blk = k_ref[pl.ds(pl.multiple_of(ka * bs, bs), bs), klanes[g]]
            scores = jnp.dot(k_blk, qT_ref[0, g, qa], preferred_element_type=F32)
            z = scores - bias_ref[g, own_a]
            z_ref[par, g] = z
            st_ref[g, pl.ds(ROW_ZMAX + par, 1), :] = jnp.max(z, axis=0, keepdims=True)

            off = slopes[g] * lax.convert_element_type((qb - kb) * bs, F32)
            chosen = jnp.logical_and(sel_ref[g, qb, pl.ds(kb, 1), :] > 0.0, s <= n_steps)
            m_in = jnp.where(own, NEG_INF, st_ref[g, pl.ds(ROW_M, 1), :])
            tile_max = st_ref[g, pl.ds(ROW_ZMAX + 1 - par, 1), :]
            m_new = jnp.where(chosen, jnp.maximum(m_in, tile_max - off), m_in)
            p_f = jnp.exp2(z_ref[1 - par, g] - jnp.where(chosen, m_new + off, -NEG_INF))
            p_ref[1 - par, g] = p_f.astype(BF16)
            st_ref[g, pl.ds(ROW_M, 1), :] = m_new
            st_ref[g, pl.ds(ROW_A + 1 - par, 1), :] = jnp.exp2(m_in - m_new)

            a = st_ref[g, pl.ds(ROW_A + par, 1), :]
            v_aug = jnp.concatenate([vT_ref[0, g, kc], ones_rows], axis=0)
            acc_new = a * acc_ref[g] + jnp.dot(v_aug, p_ref[par, g], preferred_element_type=F32)
            acc_ref[g] = acc_new
            o_ref[0, g, qc] = (acc_new[:hd] / acc_new[hd:hd + 1]).astype(o_ref.dtype)

    def unrolled(t, carry):
        for u in range(MOBA_UNROLL):
            step(MOBA_UNROLL * t + u, u % 2)
        return carry

    lax.fori_loop(0, -(-(n_steps + 2) // MOBA_UNROLL), unrolled, 0)


def moba_attention(qT, k, vT, slopes, heads_per_step=4):
    b, h, nb, hd, bs = vT.shape
    hg = heads_per_step
    steps = [(i, i if t == 0 else t - 1) for i in range(nb) for t in range(i + 1)]
    qblk = jnp.asarray([s[0] for s in steps], jnp.int32)
    kblk = jnp.asarray([s[1] for s in steps], jnp.int32)
    blk = lambda s: pl.BlockSpec((1, hg) + s, lambda bi, hi, *_: (bi, hi, 0, 0, 0))
    return pl.pallas_call(
        _moba_kernel,
        out_shape=jax.ShapeDtypeStruct(vT.shape, BF16),
        grid_spec=pltpu.PrefetchScalarGridSpec(
            num_scalar_prefetch=3,
            grid=(b, h // hg),
            in_specs=[blk((nb, qT.shape[3], bs)),
                      pl.BlockSpec((nb * bs, hg * hd), lambda bi, hi, *_: (bi, hi)),
                      blk((nb, hd, bs))],
            out_specs=blk((nb, hd, bs)),
            scratch_shapes=[pltpu.VMEM((hg, nb, nb, bs), F32),
                            pltpu.VMEM((hg, 2, bs, bs), F32),
                            pltpu.VMEM((2, hg, bs, bs), F32),
                            pltpu.VMEM((2, hg, bs, bs), BF16),
                            pltpu.VMEM((hg, hd + 16, bs), F32),
                            pltpu.VMEM((hg, 8, bs), F32)]),
        compiler_params=_cparams("parallel", "parallel"),
        name="moba_attention",
    )(slopes, qblk, kblk, qT, k, vT)


def _swa_kernel(slope_ref, sink_ref, qT_ref, k_ref, vT_ref, o_ref, pbias_ref, cbias_ref, z_ref):
    nb = qT_ref.shape[2]
    w = vT_ref.shape[4]
    gw = qT_ref.shape[4]
    slope = slope_ref[0]
    sink = sink_ref[0]
    key_i = lax.broadcasted_iota(jnp.int32, (w, gw), 0)
    qry_i = lax.broadcasted_iota(jnp.int32, (w, gw), 1) % w
    d_prev = (w + qry_i - key_i).astype(F32)
    d_cur = (qry_i - key_i).astype(F32)
    pbias_ref[...] = jnp.where(d_prev < w, slope * d_prev, -NEG_INF)
    cbias_ref[...] = jnp.where(d_cur >= 0, slope * d_cur, -NEG_INF)

    def scores(n, slot):
        qT = qT_ref[0, 0, n]
        prev = jnp.maximum(n - 1, 0)
        no_prev = jnp.where(n == 0, -NEG_INF, 0.0)
        k_prev = k_ref[pl.ds(pl.multiple_of(prev * w, w), w), :]
        k_cur = k_ref[pl.ds(pl.multiple_of(n * w, w), w), :]
        z_ref[slot, 0] = jnp.dot(k_prev, qT, preferred_element_type=F32) - pbias_ref[...] - no_prev
        z_ref[slot, 1] = jnp.dot(k_cur, qT, preferred_element_type=F32) - cbias_ref[...]

    hd = vT_ref.shape[3]
    ones_rows = jnp.ones((16, w), BF16)

    def attend(n, slot):
        zp, zc = z_ref[slot, 0], z_ref[slot, 1]
        prev = jnp.maximum(n - 1, 0)
        m = jnp.maximum(jnp.max(zp, axis=0, keepdims=True), jnp.max(zc, axis=0, keepdims=True))
        m = jnp.maximum(m, sink)
        pp = jnp.exp2(zp - m).astype(BF16)
        pc = jnp.exp2(zc - m).astype(BF16)
        v_prev = jnp.concatenate([vT_ref[0, 0, prev], ones_rows], axis=0)
        v_cur = jnp.concatenate([vT_ref[0, 0, n], ones_rows], axis=0)
        acc = jnp.dot(v_prev, pp, preferred_element_type=F32) + jnp.dot(v_cur, pc, preferred_element_type=F32)
        l = acc[hd:hd + 1] + jnp.exp2(sink - m)
        o_ref[0, 0, n] = (acc[:hd] / l).astype(o_ref.dtype)

    scores(0, 0)

    def two_blocks(t, carry):
        n = 2 * t
        scores(n + 1, 1)
        attend(n, 0)
        scores(jnp.minimum(n + 2, nb - 1), 0)
        attend(n + 1, 1)
        return carry

    lax.fori_loop(0, nb // 2, two_blocks, 0)


def swa_attention(qT, k, vT, slope_lanes, sink_lanes):
    b, kv, nb, hd, w = vT.shape
    gw = qT.shape[4]
    idx = lambda bi, ki: (bi, ki, 0, 0, 0)
    lane = pl.BlockSpec((1, 1, gw), lambda bi, ki: (ki, 0, 0))
    return pl.pallas_call(
        _swa_kernel,
        out_shape=jax.ShapeDtypeStruct((b, kv, nb, hd, gw), BF16),
        grid=(b, kv),
        in_specs=[lane, lane,
                  pl.BlockSpec((1, 1, nb, qT.shape[3], gw), idx),
                  pl.BlockSpec((nb * w, k.shape[1]), lambda bi, ki: (bi, 0)),
                  pl.BlockSpec((1, 1, nb, hd, w), idx)],
        out_specs=pl.BlockSpec((1, 1, nb, hd, gw), idx),
        scratch_shapes=[pltpu.VMEM((w, gw), F32), pltpu.VMEM((w, gw), F32), pltpu.VMEM((2, 2, w, gw), F32)],
        compiler_params=_cparams("parallel", "parallel"),
        name="swa_attention",
    )(slope_lanes, sink_lanes, qT, k, vT)


def _swiglu(xb, wg, wu, wd):
    hg = jnp.dot(xb, wg, preferred_element_type=F32)
    hu = jnp.dot(xb, wu, preferred_element_type=F32)
    hh = hg * jax.nn.sigmoid(hg) * hu
    return jnp.dot(hh.astype(BF16), wd, preferred_element_type=F32)


def _route(x, wrT):
    tt = x.shape[0]
    logits = lax.dot_general(wrT, x, _NT, preferred_element_type=F32, precision=lax.Precision.HIGHEST)
    e_i = lax.broadcasted_iota(jnp.int32, logits.shape, 0)
    work = logits
    vals, picks = [], []
    for _ in range(MOE_TOPK):
        best = jnp.max(work, axis=0, keepdims=True)
        first = jnp.min(jnp.where(work == best, e_i, N_EXPERTS), axis=0, keepdims=True)
        pick = e_i == first
        vals.append(best)
        picks.append(pick)
        work = jnp.where(pick, PICKED, work)
    ex = [jnp.exp(v - vals[0]) for v in vals]
    denom = functools.reduce(lambda a, c: a + c, ex)
    gates = jnp.zeros(logits.shape, F32)
    sel = jnp.zeros(logits.shape, F32)
    for pick, e in zip(picks, ex):
        gates = jnp.where(pick, e / denom, gates)
        sel = jnp.where(pick, 1.0, sel)
    before = (lax.broadcasted_iota(jnp.int32, (tt, tt), 0) < lax.broadcasted_iota(jnp.int32, (tt, tt), 1))
    rank = jnp.dot(sel.astype(BF16), jnp.where(before, 1.0, 0.0).astype(BF16), preferred_element_type=F32)
    return gates, jnp.where(sel > 0.0, rank, -1.0), jnp.sum(sel, axis=1, keepdims=True)


MOE_BIG = 512
MOE_SMALL = (256, 288, 320)
MOE_TOKEN_TILE = 1024
MOE_FF_SPLIT = 2


def _moe_kernel(count_ref, x_ref, gate_ref, rank_ref, wg_ref, wu_ref, wd_ref, g_ref, b_ref, o_ref,
                xb_ref, xe_ref, ye_ref):
    t, e, half = pl.program_id(0), pl.program_id(1), pl.program_id(2)
    ne, nh = pl.num_programs(1), pl.num_programs(2)
    tt = x_ref.shape[0]
    count = count_ref[t * ne + e]
    n_big = count // MOE_BIG
    rest = count - n_big * MOE_BIG
    rank_row = rank_ref[pl.ds(e, 1), :]

    def one_hot(r0, m):
        row_i = lax.broadcasted_iota(jnp.int32, (m, tt), 0).astype(F32)
        return rank_row == row_i + lax.convert_element_type(r0, F32)

    def row_groups(fn, sizes=MOE_SMALL[-1:]):
        def body(i, carry):
            fn(pl.multiple_of(i * MOE_BIG, MOE_BIG), MOE_BIG)
            return carry
        lax.fori_loop(0, n_big + (rest > MOE_SMALL[-1]).astype(jnp.int32), body, 0)
        lo = 0
        for m in sizes:
            @pl.when(jnp.logical_and(rest > lo, rest <= m))
            def _():
                fn(pl.multiple_of(n_big * MOE_BIG, MOE_BIG), m)
            lo = m

    @pl.when(jnp.logical_and(e == 0, half == 0))
    def _():
        o_ref[...] = jnp.zeros_like(o_ref)
        xb_ref[...] = x_ref[...].astype(BF16)

    @pl.when(half == 0)
    def _():
        def gather(r0, m):
            hot = jnp.where(one_hot(r0, m), 1.0, 0.0).astype(BF16)
            xe_ref[pl.ds(r0, m), :] = jnp.dot(hot, xb_ref[...], preferred_element_type=F32).astype(BF16)
            ye_ref[pl.ds(r0, m), :] = jnp.zeros((m, ye_ref.shape[1]), F32)
        row_groups(gather)

    def expert(r0, m):
        ye_ref[pl.ds(r0, m), :] += _swiglu(xe_ref[pl.ds(r0, m), :], wg_ref[0], wu_ref[0], wd_ref[0])
    row_groups(expert, MOE_SMALL)

    @pl.when(half == nh - 1)
    def _():
        gate_row = gate_ref[pl.ds(e, 1), :]

        def scatter(r0, m):
            hot = one_hot(r0, m)
            w_col = jnp.sum(jnp.where(hot, gate_row, 0.0), axis=1, keepdims=True)
            yw = (ye_ref[pl.ds(r0, m), :] * w_col).astype(BF16)
            o_ref[...] += lax.dot_general(jnp.where(hot, 1.0, 0.0).astype(BF16), yw,
                                          (((0,), (0,)), ((), ())), preferred_element_type=F32)
        row_groups(scatter)

    @pl.when(jnp.logical_and(e == ne - 1, half == nh - 1))
    def _():
        y = DEEPNORM_ALPHA * x_ref[...] + o_ref[...]
        o_ref[...] = _layer_norm(y, g_ref[...], b_ref[...])


def moe_ffn_ln(x, gate, rank, count, wg, wu, wd, g, b, tt):
    n, d = x.shape
    ne, fh = wg.shape[0], wg.shape[2] // MOE_FF_SPLIT
    w_cols = pl.BlockSpec((1, d, fh), lambda t, e, h, *_: (e, 0, h))
    w_rows = pl.BlockSpec((1, fh, d), lambda t, e, h, *_: (e, h, 0))
    row = pl.BlockSpec((1, d), lambda t, e, h, *_: (0, 0))
    return pl.pallas_call(
        _moe_kernel,
        out_shape=jax.ShapeDtypeStruct((n, d), F32),
        grid_spec=pltpu.PrefetchScalarGridSpec(
            num_scalar_prefetch=1,
            grid=(n // tt, ne, MOE_FF_SPLIT),
            in_specs=[pl.BlockSpec((tt, d), lambda t, e, h, *_: (t, 0)),
                      pl.BlockSpec((ne, tt), lambda t, e, h, *_: (0, t)),
                      pl.BlockSpec((ne, tt), lambda t, e, h, *_: (0, t)),
                      w_cols, w_cols, w_rows, row, row],
            out_specs=pl.BlockSpec((tt, d), lambda t, e, h, *_: (t, 0)),
            scratch_shapes=[pltpu.VMEM((tt, d), BF16), pltpu.VMEM((tt, d), BF16), pltpu.VMEM((tt, d), F32)]),
        compiler_params=_cparams("parallel", "arbitrary", "arbitrary"),
        name="moe_ffn_ln",
    )(count, x, gate, rank, wg, wu, wd, g.reshape(1, d), b.reshape(1, d))


def _alibi_slopes(n):
    return jnp.exp2(-8.0 * jnp.arange(1, n + 1, dtype=F32) / n)


def kernel(x, w_qkv_a, w_o_a, w_kv_shared, w_q_b, w_o_b, sinks_b, w_gate_d, w_up_d, w_down_d,
           w_router, w_gate_e, w_up_e, w_down_e, ln_gain, ln_bias):
    B, T, D = x.shape
    hd = HEAD_DIM
    H = D // hd
    N = B * T
    xs = x.reshape(N, D)
    q_scale = hd ** -0.5

    HA = w_qkv_a.shape[-1] // 3
    wq, wk, wv = (w_qkv_a[0][:, i * HA:(i + 1) * HA] for i in range(3))
    qT, k, vT = qkv_proj(xs, (wq * (q_scale * LOG2E)).T.astype(BF16), wk.astype(BF16), wv.T.astype(BF16),
                         batch=B, hq=H, hv=H, hd=hd, bs=MOBA_BLOCK, pad=2, q_groups=1)
    oT = moba_attention(qT, k, vT, _alibi_slopes(H))
    dense = (w_gate_d[0].astype(BF16), w_up_d[0].astype(BF16), w_down_d[0].astype(BF16), ln_gain[0, 1], ln_bias[0, 1])
    xs = oT_proj_ln(oT, w_o_a[0].astype(BF16), xs, ln_gain[0, 0], ln_bias[0, 0], MOBA_BLOCK, 512, ffn=dense)

    KV = N_KV_B
    G = H // KV
    W = WINDOW
    wk, wv = w_kv_shared[:, :KV * hd], w_kv_shared[:, KV * hd:]
    qT, k, vT = qkv_proj(xs, (w_q_b[0] * (q_scale * LOG2E)).T.astype(BF16), wk.astype(BF16), wv.T.astype(BF16),
                         batch=B, hq=KV, hv=KV, hd=hd, bs=W, pad=KV, q_groups=G)
    slope_lanes = jnp.repeat(_alibi_slopes(H).reshape(KV, G) * LOG2E, W, axis=1).reshape(KV, 1, G * W)
    sink_lanes = jnp.repeat(sinks_b[0].astype(F32).reshape(KV, G) * LOG2E, W, axis=1).reshape(KV, 1, G * W)
    oT = swa_attention(qT, k, vT, slope_lanes, sink_lanes)
    tt = min(MOE_TOKEN_TILE, T)
    xs, gate, rank, count = oT_proj_ln(oT, w_o_b[0].astype(BF16), xs, ln_gain[1, 0], ln_bias[1, 0], W, tt,
                                       router_wT=w_router[0].T)

    count = count[:, :, 0].astype(jnp.int32).reshape(-1)
    xs = moe_ffn_ln(xs, gate, rank, count, w_gate_e[0].astype(BF16), w_up_e[0].astype(BF16),
                    w_down_e[0].astype(BF16), ln_gain[1, 1], ln_bias[1, 1], tt)
    return xs.reshape(B, T, D)
```

```python
import functools

import jax
import jax.numpy as jnp
from jax import lax
from jax.experimental import pallas as pl
from jax.experimental.pallas import tpu as pltpu

HEAD_DIM = 64
MOBA_BLOCK = 256
MOBA_TOPK = 3
MOBA_UNROLL = 4
PROJ_ROWS = 256
N_KV_B = 2
WINDOW = 128
N_EXPERTS = 8
MOE_TOPK = 2
DEPTH = 2
DEEPNORM_ALPHA = (2 * DEPTH) ** 0.25
LN_EPS = 1e-5
NEG_INF = -1e30
PICKED = -3e38
LOG2E = 1.4426950408889634
VMEM_LIMIT = 56 * 1024 * 1024

F32 = jnp.float32
BF16 = jnp.bfloat16


def _cparams(*sem):
    return pltpu.CompilerParams(dimension_semantics=sem, vmem_limit_bytes=VMEM_LIMIT)


def _layer_norm(y, g, b):
    mu = jnp.mean(y, axis=-1, keepdims=True)
    yc = y - mu
    var = jnp.mean(yc * yc, axis=-1, keepdims=True)
    return yc * lax.rsqrt(var + LN_EPS) * g + b


_NT = (((1,), (1,)), ((), ()))
_TN = (((0,), (0,)), ((), ()))


def _qkv_kernel(x_ref, wqT_ref, wk_ref, wvT_ref, qT_ref, k_ref, vT_ref, *, q_groups):
    xb = x_ref[...].astype(BF16)
    k_ref[...] = jnp.dot(xb, wk_ref[...], preferred_element_type=F32).astype(BF16)
    qT = lax.dot_general(wqT_ref[...], xb, _NT, preferred_element_type=F32).astype(BF16)
    vT = lax.dot_general(wvT_ref[...], xb, _NT, preferred_element_type=F32).astype(BF16)
    _, hv, nblk, hd, bs = vT_ref.shape
    _, hq, _, padded, ql = qT_ref.shape
    pad = padded // hd
    zero = jnp.zeros((hd, bs), BF16)
    for blk in range(nblk):
        cols = slice(blk * bs, (blk + 1) * bs)
        for h in range(hv):
            vT_ref[0, h, blk] = vT[h * hd:(h + 1) * hd, cols]
        for h in range(hq):
            lanes = []
            for g in range(q_groups):
                r0 = (h * q_groups + g) * hd
                parts = [zero] * pad
                parts[h % pad] = qT[r0:r0 + hd, cols]
                lanes.append(jnp.concatenate(parts, axis=0))
            qT_ref[0, h, blk] = lanes[0] if q_groups == 1 else jnp.concatenate(lanes, axis=1)


def qkv_proj(x, wqT, wk, wvT, batch, hq, hv, hd, bs, pad, q_groups, tm=1024):
    n, d = x.shape
    t = n // batch
    tm = min(tm, t)
    nblk = tm // bs
    steps = t // tm
    whole = lambda a: pl.BlockSpec(a.shape, lambda b, i: (0, 0))
    return pl.pallas_call(
        functools.partial(_qkv_kernel, q_groups=q_groups),
        out_shape=(jax.ShapeDtypeStruct((batch, hq, t // bs, pad * hd, q_groups * bs), BF16),
                   jax.ShapeDtypeStruct((n, wk.shape[1]), BF16),
                   jax.ShapeDtypeStruct((batch, hv, t // bs, hd, bs), BF16)),
        grid=(batch, steps),
        in_specs=[pl.BlockSpec((tm, d), lambda b, i: (b * steps + i, 0)), whole(wqT), whole(wk), whole(wvT)],
        out_specs=(pl.BlockSpec((1, hq, nblk, pad * hd, q_groups * bs), lambda b, i: (b, 0, i, 0, 0)),
                   pl.BlockSpec((tm, wk.shape[1]), lambda b, i: (b * steps + i, 0)),
                   pl.BlockSpec((1, hv, nblk, hd, bs), lambda b, i: (b, 0, i, 0, 0))),
        compiler_params=_cparams("parallel", "parallel"),
        name="qkv_proj",
    )(x, wqT, wk, wvT)


def _oT_proj_ln_kernel(oT_ref, w_ref, x_ref, g_ref, b_ref, *rest, bs, then):
    if then == "ffn":
        wg_ref, wu_ref, wd_ref, g2_ref, b2_ref, o_ref = rest
    elif then == "route":
        wrT_ref, o_ref, gate_ref, rank_ref, count_ref = rest
    else:
        o_ref, = rest
    _, ho, nblk, hd, lanes = oT_ref.shape
    per = max(1, PROJ_ROWS // bs)
    mixes = []
    for first in range(0, nblk, per):
        tiles = []
        for blk in range(first, first + per):
            heads = [oT_ref[0, h, blk, :, g * bs:(g + 1) * bs] for h in range(ho) for g in range(lanes // bs)]
            tiles.append(jnp.concatenate(heads, axis=0))
        aT = tiles[0] if per == 1 else jnp.concatenate(tiles, axis=1)
        mixes.append(lax.dot_general(aT, w_ref[...], _TN, preferred_element_type=F32))
    for i, mix in enumerate(mixes):
        rows = pl.ds(i * per * bs, per * bs)
        y = DEEPNORM_ALPHA * x_ref[rows, :] + mix
        o_ref[rows, :] = _layer_norm(y, g_ref[...], b_ref[...])
    if then == "ffn":
        x1 = o_ref[...]
        ffn = _swiglu(x1.astype(BF16), wg_ref[...], wu_ref[...], wd_ref[...])
        o_ref[...] = _layer_norm(DEEPNORM_ALPHA * x1 + ffn, g2_ref[...], b2_ref[...])
    elif then == "route":
        gate_ref[...], rank_ref[...], count = _route(o_ref[...], wrT_ref[...])
        count_ref[0] = jnp.broadcast_to(count, count_ref.shape[1:])


def oT_proj_ln(oT, w, x, g, b, bs, tm, ffn=None, router_wT=None):
    batch, ho, nb, hd, lanes = oT.shape
    n, d = x.shape
    t = n // batch
    tm = min(tm, t)
    nblk = tm // bs
    steps = t // tm
    tile = pl.BlockSpec((tm, d), lambda bi, i: (bi * steps + i, 0))
    const = lambda a: pl.BlockSpec(a.shape, lambda bi, i: (0,) * a.ndim, pipeline_mode=pl.Buffered(1))
    args = [oT, w, x, g.reshape(1, d), b.reshape(1, d)]
    in_specs = [pl.BlockSpec((1, ho, nblk, hd, lanes), lambda bi, i: (bi, 0, i, 0, 0)), const(w), tile,
                const(args[3]), const(args[4])]
    out_shape, out_specs, then = jax.ShapeDtypeStruct((n, d), F32), tile, None
    if ffn is not None:
        wg, wu, wd, g2, b2 = ffn
        extra = [wg, wu, wd, g2.reshape(1, d), b2.reshape(1, d)]
        then = "ffn"
    elif router_wT is not None:
        ne = router_wT.shape[0]
        extra = [router_wT]
        then = "route"
        out_shape = (out_shape, jax.ShapeDtypeStruct((ne, n), F32), jax.ShapeDtypeStruct((ne, n), F32),
                     jax.ShapeDtypeStruct((n // tm, ne, 128), F32))
        lane_tile = pl.BlockSpec((ne, tm), lambda bi, i: (0, bi * steps + i))
        out_specs = (tile, lane_tile, lane_tile, pl.BlockSpec((1, ne, 128), lambda bi, i: (bi * steps + i, 0, 0)))
    else:
        extra = []
    return pl.pallas_call(
        functools.partial(_oT_proj_ln_kernel, bs=bs, then=then),
        out_shape=out_shape,
        grid=(batch, steps),
        in_specs=in_specs + [const(a) for a in extra],
        out_specs=out_specs,
        compiler_params=_cparams("parallel", "parallel"),
        name="proj_ln" + ("_" + then if then else ""),
    )(*args, *extra)


def _moba_kernel(slope_ref, qblk_ref, kblk_ref, qT_ref, k_ref, vT_ref, o_ref,
                 sel_ref, bias_ref, z_ref, p_ref, acc_ref, st_ref):
    _, hg, nb, hd, bs = vT_ref.shape
    pair = qT_ref.shape[3]
    n_steps = nb * (nb + 1) // 2
    heads = range(hg)
    slopes = [slope_ref[pl.program_id(1) * hg + g] * LOG2E for g in heads]

    key_i = lax.broadcasted_iota(jnp.int32, (bs, bs), 0)
    qry_i = lax.broadcasted_iota(jnp.int32, (bs, bs), 1)
    rel = (qry_i - key_i).astype(F32)
    blk_i = lax.broadcasted_iota(jnp.int32, (nb, bs), 0)
    klanes = [pl.ds((g * hd) // pair * pair, pair) for g in heads]
    kmean = jnp.sum(k_ref[...].astype(F32).reshape(nb, bs, hg * hd), axis=1) * (1.0 / bs)
    km_hi = kmean.astype(BF16)
    km_lo = (kmean - km_hi.astype(F32)).astype(BF16)
    km2 = []
    for g in heads:
        bias = slopes[g] * rel
        bias_ref[g, 0] = bias
        bias_ref[g, 1] = jnp.where(rel >= 0, bias, -NEG_INF)
        lo = (g * hd) // pair * pair
        km2.append(jnp.concatenate([km_hi[:, lo:lo + pair], km_lo[:, lo:lo + pair]], axis=0))

    def select(i, carry):
        past = blk_i < i
        own = jnp.where(blk_i == i, 1.0, 0.0)
        for g in heads:
            g2 = jnp.dot(km2[g], qT_ref[0, g, i], preferred_element_type=F32)
            gate = jnp.where(past, g2[:nb] + g2[nb:], NEG_INF)
            sel = jnp.zeros((nb, bs), F32)
            for _ in range(MOBA_TOPK):
                best = jnp.max(gate, axis=0, keepdims=True)
                first = jnp.min(jnp.where(gate == best, blk_i, nb), axis=0, keepdims=True)
                pick = blk_i == first
                sel = jnp.where(pick, 1.0, sel)
                gate = jnp.where(pick, PICKED, gate)
            sel_ref[g, i] = jnp.where(past, sel, own)
        return carry

    lax.fori_loop(0, nb, select, 0, unroll=2)

    ROW_M, ROW_A, ROW_ZMAX = 0, 1, 3
    ones_rows = jnp.ones((acc_ref.shape[1] - hd, bs), BF16)
    z_ref[...] = jnp.zeros_like(z_ref)
    p_ref[...] = jnp.zeros_like(p_ref)
    acc0 = jnp.where(lax.broadcasted_iota(jnp.int32, acc_ref.shape[1:], 0) < hd, 0.0, 1.0)
    st0 = jnp.where(lax.broadcasted_iota(jnp.int32, (8, bs), 0) == ROW_M, NEG_INF, 1.0)
    for g in heads:
        acc_ref[g] = acc0
        st_ref[g] = st0

    def step(s, par):
        sa = jnp.minimum(s, n_steps - 1)
        sb = jnp.clip(s - 1, 0, n_steps - 1)
        sc = jnp.clip(s - 2, 0, n_steps - 1)
        qa, ka = qblk_ref[sa], kblk_ref[sa]
        qb, kb = qblk_ref[sb], kblk_ref[sb]
        qc, kc = qblk_ref[sc], kblk_ref[sc]
        own_a = (qa == ka).astype(jnp.int32)
        own = qb == kb
        for g in heads:
            k_blk = k_ref[pl.ds(pl.multiple_of(ka * bs, bs), bs), klanes[g]]
            scores = jnp.dot(k_blk, qT_ref[0, g, qa], preferred_element_type=F32)
            z = scores - bias_ref[g, own_a]
            z_ref[par, g] = z
            st_ref[g, pl.ds(ROW_ZMAX + par, 1), :] = jnp.max(z, axis=0, keepdims=True)

            off = slopes[g] * lax.convert_element_type((qb - kb) * bs, F32)
            chosen = jnp.logical_and(sel_ref[g, qb, pl.ds(kb, 1), :] > 0.0, s <= n_steps)
            m_in = jnp.where(own, NEG_INF, st_ref[g, pl.ds(ROW_M, 1), :])
            tile_max = st_ref[g, pl.ds(ROW_ZMAX + 1 - par, 1), :]
            m_new = jnp.where(chosen, jnp.maximum(m_in, tile_max - off), m_in)
            p_f = jnp.exp2(z_ref[1 - par, g] - jnp.where(chosen, m_new + off, -NEG_INF))
            p_ref[1 - par, g] = p_f.astype(BF16)
            st_ref[g, pl.ds(ROW_M, 1), :] = m_new
            st_ref[g, pl.ds(ROW_A + 1 - par, 1), :] = jnp.exp2(m_in - m_new)

            a = st_ref[g, pl.ds(ROW_A + par, 1), :]
            v_aug = jnp.concatenate([vT_ref[0, g, kc], ones_rows], axis=0)
            acc_new = a * acc_ref[g] + jnp.dot(v_aug, p_ref[par, g], preferred_element_type=F32)
            acc_ref[g] = acc_new
            o_ref[0, g, qc] = (acc_new[:hd] / acc_new[hd:hd + 1]).astype(o_ref.dtype)

    def unrolled(t, carry):
        for u in range(MOBA_UNROLL):
            step(MOBA_UNROLL * t + u, u % 2)
        return carry

    lax.fori_loop(0, -(-(n_steps + 2) // MOBA_UNROLL), unrolled, 0)


def moba_attention(qT, k, vT, slopes, heads_per_step=4):
    b, h, nb, hd, bs = vT.shape
    hg = heads_per_step
    steps = [(i, i if t == 0 else t - 1) for i in range(nb) for t in range(i + 1)]
    qblk = jnp.asarray([s[0] for s in steps], jnp.int32)
    kblk = jnp.asarray([s[1] for s in steps], jnp.int32)
    blk = lambda s: pl.BlockSpec((1, hg) + s, lambda bi, hi, *_: (bi, hi, 0, 0, 0))
    return pl.pallas_call(
        _moba_kernel,
        out_shape=jax.ShapeDtypeStruct(vT.shape, BF16),
        grid_spec=pltpu.PrefetchScalarGridSpec(
            num_scalar_prefetch=3,
            grid=(b, h // hg),
            in_specs=[blk((nb, qT.shape[3], bs)),
                      pl.BlockSpec((nb * bs, hg * hd), lambda bi, hi, *_: (bi, hi)),
                      blk((nb, hd, bs))],
            out_specs=blk((nb, hd, bs)),
            scratch_shapes=[pltpu.VMEM((hg, nb, nb, bs), F32),
                            pltpu.VMEM((hg, 2, bs, bs), F32),
                            pltpu.VMEM((2, hg, bs, bs), F32),
                            pltpu.VMEM((2, hg, bs, bs), BF16),
                            pltpu.VMEM((hg, hd + 16, bs), F32),
                            pltpu.VMEM((hg, 8, bs), F32)]),
        compiler_params=_cparams("parallel", "parallel"),
        name="moba_attention",
    )(slopes, qblk, kblk, qT, k, vT)


def _swa_kernel(slope_ref, sink_ref, qT_ref, k_ref, vT_ref, o_ref, pbias_ref, cbias_ref, z_ref):
    nb = qT_ref.shape[2]
    w = vT_ref.shape[4]
    gw = qT_ref.shape[4]
    slope = slope_ref[0]
    sink = sink_ref[0]
    key_i = lax.broadcasted_iota(jnp.int32, (w, gw), 0)
    qry_i = lax.broadcasted_iota(jnp.int32, (w, gw), 1) % w
    d_prev = (w + qry_i - key_i).astype(F32)
    d_cur = (qry_i - key_i).astype(F32)
    pbias_ref[...] = jnp.where(d_prev < w, slope * d_prev, -NEG_INF)
    cbias_ref[...] = jnp.where(d_cur >= 0, slope * d_cur, -NEG_INF)

    def scores(n, slot):
        qT = qT_ref[0, 0, n]
        prev = jnp.maximum(n - 1, 0)
        no_prev = jnp.where(n == 0, -NEG_INF, 0.0)
        k_prev = k_ref[pl.ds(pl.multiple_of(prev * w, w), w), :]
        k_cur = k_ref[pl.ds(pl.multiple_of(n * w, w), w), :]
        z_ref[slot, 0] = jnp.dot(k_prev, qT, preferred_element_type=F32) - pbias_ref[...] - no_prev
        z_ref[slot, 1] = jnp.dot(k_cur, qT, preferred_element_type=F32) - cbias_ref[...]

    hd = vT_ref.shape[3]
    ones_rows = jnp.ones((16, w), BF16)

    def attend(n, slot):
        zp, zc = z_ref[slot, 0], z_ref[slot, 1]
        prev = jnp.maximum(n - 1, 0)
        m = jnp.maximum(jnp.max(zp, axis=0, keepdims=True), jnp.max(zc, axis=0, keepdims=True))
        m = jnp.maximum(m, sink)
        pp = jnp.exp2(zp - m).astype(BF16)
        pc = jnp.exp2(zc - m).astype(BF16)
        v_prev = jnp.concatenate([vT_ref[0, 0, prev], ones_rows], axis=0)
        v_cur = jnp.concatenate([vT_ref[0, 0, n], ones_rows], axis=0)
        acc = jnp.dot(v_prev, pp, preferred_element_type=F32) + jnp.dot(v_cur, pc, preferred_element_type=F32)
        l = acc[hd:hd + 1] + jnp.exp2(sink - m)
        o_ref[0, 0, n] = (acc[:hd] / l).astype(o_ref.dtype)

    scores(0, 0)

    def two_blocks(t, carry):
        n = 2 * t
        scores(n + 1, 1)
        attend(n, 0)
        scores(jnp.minimum(n + 2, nb - 1), 0)
        attend(n + 1, 1)
        return carry

    lax.fori_loop(0, nb // 2, two_blocks, 0)


def swa_attention(qT, k, vT, slope_lanes, sink_lanes):
    b, kv, nb, hd, w = vT.shape
    gw = qT.shape[4]
    idx = lambda bi, ki: (bi, ki, 0, 0, 0)
    lane = pl.BlockSpec((1, 1, gw), lambda bi, ki: (ki, 0, 0))
    return pl.pallas_call(
        _swa_kernel,
        out_shape=jax.ShapeDtypeStruct((b, kv, nb, hd, gw), BF16),
        grid=(b, kv),
        in_specs=[lane, lane,
                  pl.BlockSpec((1, 1, nb, qT.shape[3], gw), idx),
                  pl.BlockSpec((nb * w, k.shape[1]), lambda bi, ki: (bi, 0)),
                  pl.BlockSpec((1, 1, nb, hd, w), idx)],
        out_specs=pl.BlockSpec((1, 1, nb, hd, gw), idx),
        scratch_shapes=[pltpu.VMEM((w, gw), F32), pltpu.VMEM((w, gw), F32), pltpu.VMEM((2, 2, w, gw), F32)],
        compiler_params=_cparams("parallel", "parallel"),
        name="swa_attention",
    )(slope_lanes, sink_lanes, qT, k, vT)


def _swiglu(xb, wg, wu, wd):
    hg = jnp.dot(xb, wg, preferred_element_type=F32)
    hu = jnp.dot(xb, wu, preferred_element_type=F32)
    hh = hg * jax.nn.sigmoid(hg) * hu
    return jnp.dot(hh.astype(BF16), wd, preferred_element_type=F32)


def _route(x, wrT):
    tt = x.shape[0]
    logits = lax.dot_general(wrT, x, _NT, preferred_element_type=F32, precision=lax.Precision.HIGHEST)
    e_i = lax.broadcasted_iota(jnp.int32, logits.shape, 0)
    work = logits
    vals, picks = [], []
    for _ in range(MOE_TOPK):
        best = jnp.max(work, axis=0, keepdims=True)
        first = jnp.min(jnp.where(work == best, e_i, N_EXPERTS), axis=0, keepdims=True)
        pick = e_i == first
        vals.append(best)
        picks.append(pick)
        work = jnp.where(pick, PICKED, work)
    ex = [jnp.exp(v - vals[0]) for v in vals]
    denom = functools.reduce(lambda a, c: a + c, ex)
    gates = jnp.zeros(logits.shape, F32)
    sel = jnp.zeros(logits.shape, F32)
    for pick, e in zip(picks, ex):
        gates = jnp.where(pick, e / denom, gates)
        sel = jnp.where(pick, 1.0, sel)
    before = (lax.broadcasted_iota(jnp.int32, (tt, tt), 0) < lax.broadcasted_iota(jnp.int32, (tt, tt), 1))
    rank = jnp.dot(sel.astype(BF16), jnp.where(before, 1.0, 0.0).astype(BF16), preferred_element_type=F32)
    return gates, jnp.where(sel > 0.0, rank, -1.0), jnp.sum(sel, axis=1, keepdims=True)


MOE_BIG = 512
MOE_SMALL = (256, 288, 320)
MOE_TOKEN_TILE = 1024
MOE_FF_SPLIT = 2


def _moe_kernel(count_ref, x_ref, gate_ref, rank_ref, wg_ref, wu_ref, wd_ref, g_ref, b_ref, o_ref,
                xb_ref, xe_ref, ye_ref):
    t, e, half = pl.program_id(0), pl.program_id(1), pl.program_id(2)
    ne, nh = pl.num_programs(1), pl.num_programs(2)
    tt = x_ref.shape[0]
    count = count_ref[t * ne + e]
    n_big = count // MOE_BIG
    rest = count - n_big * MOE_BIG
    rank_row = rank_ref[pl.ds(e, 1), :]

    def one_hot(r0, m):
        row_i = lax.broadcasted_iota(jnp.int32, (m, tt), 0).astype(F32)
        return rank_row == row_i + lax.convert_element_type(r0, F32)

    def row_groups(fn, sizes=MOE_SMALL[-1:]):
        def body(i, carry):
            fn(pl.multiple_of(i * MOE_BIG, MOE_BIG), MOE_BIG)
            return carry
        lax.fori_loop(0, n_big + (rest > MOE_SMALL[-1]).astype(jnp.int32), body, 0)
        lo = 0
        for m in sizes:
            @pl.when(jnp.logical_and(rest > lo, rest <= m))
            def _():
                fn(pl.multiple_of(n_big * MOE_BIG, MOE_BIG), m)
            lo = m

    @pl.when(jnp.logical_and(e == 0, half == 0))
    def _():
        o_ref[...] = jnp.zeros_like(o_ref)
        xb_ref[...] = x_ref[...].astype(BF16)

    @pl.when(half == 0)
    def _():
        def gather(r0, m):
            hot = jnp.where(one_hot(r0, m), 1.0, 0.0).astype(BF16)
            xe_ref[pl.ds(r0, m), :] = jnp.dot(hot, xb_ref[...], preferred_element_type=F32).astype(BF16)
            ye_ref[pl.ds(r0, m), :] = jnp.zeros((m, ye_ref.shape[1]), F32)
        row_groups(gather)

    def expert(r0, m):
        ye_ref[pl.ds(r0, m), :] += _swiglu(xe_ref[pl.ds(r0, m), :], wg_ref[0], wu_ref[0], wd_ref[0])
    row_groups(expert, MOE_SMALL)

    @pl.when(half == nh - 1)
    def _():
        gate_row = gate_ref[pl.ds(e, 1), :]

        def scatter(r0, m):
            hot = one_hot(r0, m)
            w_col = jnp.sum(jnp.where(hot, gate_row, 0.0), axis=1, keepdims=True)
            yw = (ye_ref[pl.ds(r0, m), :] * w_col).astype(BF16)
            o_ref[...] += lax.dot_general(jnp.where(hot, 1.0, 0.0).astype(BF16), yw,
                                          (((0,), (0,)), ((), ())), preferred_element_type=F32)
        row_groups(scatter, (MOE_SMALL[0], MOE_SMALL[-1]))

    @pl.when(jnp.logical_and(e == ne - 1, half == nh - 1))
    def _():
        y = DEEPNORM_ALPHA * x_ref[...] + o_ref[...]
        o_ref[...] = _layer_norm(y, g_ref[...], b_ref[...])


def moe_ffn_ln(x, gate, rank, count, wg, wu, wd, g, b, tt):
    n, d = x.shape
    ne, fh = wg.shape[0], wg.shape[2] // MOE_FF_SPLIT
    w_cols = pl.BlockSpec((1, d, fh), lambda t, e, h, *_: (e, 0, h))
    w_rows = pl.BlockSpec((1, fh, d), lambda t, e, h, *_: (e, h, 0))
    row = pl.BlockSpec((1, d), lambda t, e, h, *_: (0, 0))
    return pl.pallas_call(
        _moe_kernel,
        out_shape=jax.ShapeDtypeStruct((n, d), F32),
        grid_spec=pltpu.PrefetchScalarGridSpec(
            num_scalar_prefetch=1,
            grid=(n // tt, ne, MOE_FF_SPLIT),
            in_specs=[pl.BlockSpec((tt, d), lambda t, e, h, *_: (t, 0)),
                      pl.BlockSpec((ne, tt), lambda t, e, h, *_: (0, t)),
                      pl.BlockSpec((ne, tt), lambda t, e, h, *_: (0, t)),
                      w_cols, w_cols, w_rows, row, row],
            out_specs=pl.BlockSpec((tt, d), lambda t, e, h, *_: (t, 0)),
            scratch_shapes=[pltpu.VMEM((tt, d), BF16), pltpu.VMEM((tt, d), BF16), pltpu.VMEM((tt, d), F32)]),
        compiler_params=_cparams("parallel", "arbitrary", "arbitrary"),
        name="moe_ffn_ln",
    )(count, x, gate, rank, wg, wu, wd, g.reshape(1, d), b.reshape(1, d))


def _alibi_slopes(n):
    return jnp.exp2(-8.0 * jnp.arange(1, n + 1, dtype=F32) / n)


def kernel(x, w_qkv_a, w_o_a, w_kv_shared, w_q_b, w_o_b, sinks_b, w_gate_d, w_up_d, w_down_d,
           w_router, w_gate_e, w_up_e, w_down_e, ln_gain, ln_bias):
    B, T, D = x.shape
    hd = HEAD_DIM
    H = D // hd
    N = B * T
    xs = x.reshape(N, D)
    q_scale = hd ** -0.5

    HA = w_qkv_a.shape[-1] // 3
    wq, wk, wv = (w_qkv_a[0][:, i * HA:(i + 1) * HA] for i in range(3))
    qT, k, vT = qkv_proj(xs, (wq * (q_scale * LOG2E)).T.astype(BF16), wk.astype(BF16), wv.T.astype(BF16),
                         batch=B, hq=H, hv=H, hd=hd, bs=MOBA_BLOCK, pad=2, q_groups=1)
    oT = moba_attention(qT, k, vT, _alibi_slopes(H))
    dense = (w_gate_d[0].astype(BF16), w_up_d[0].astype(BF16), w_down_d[0].astype(BF16), ln_gain[0, 1], ln_bias[0, 1])
    xs = oT_proj_ln(oT, w_o_a[0].astype(BF16), xs, ln_gain[0, 0], ln_bias[0, 0], MOBA_BLOCK, 512, ffn=dense)

    KV = N_KV_B
    G = H // KV
    W = WINDOW
    wk, wv = w_kv_shared[:, :KV * hd], w_kv_shared[:, KV * hd:]
    qT, k, vT = qkv_proj(xs, (w_q_b[0] * (q_scale * LOG2E)).T.astype(BF16), wk.astype(BF16), wv.T.astype(BF16),
                         batch=B, hq=KV, hv=KV, hd=hd, bs=W, pad=KV, q_groups=G)
    slope_lanes = jnp.repeat(_alibi_slopes(H).reshape(KV, G) * LOG2E, W, axis=1).reshape(KV, 1, G * W)
    sink_lanes = jnp.repeat(sinks_b[0].astype(F32).reshape(KV, G) * LOG2E, W, axis=1).reshape(KV, 1, G * W)
    oT = swa_attention(qT, k, vT, slope_lanes, sink_lanes)
    tt = min(MOE_TOKEN_TILE, T)
    xs, gate, rank, count = oT_proj_ln(oT, w_o_b[0].astype(BF16), xs, ln_gain[1, 0], ln_bias[1, 0], W, tt,
                                       router_wT=w_router[0].T)

    count = count[:, :, 0].astype(jnp.int32).reshape(-1)
    xs = moe_ffn_ln(xs, gate, rank, count, w_gate_e[0].astype(BF16), w_up_e[0].astype(BF16),
                    w_down_e[0].astype(BF16), ln_gain[1, 1], ln_bias[1, 1], tt)
    return xs.reshape(B, T, D)
```

```python
import functools

import jax
import jax.numpy as jnp
from jax import lax
from jax.experimental import pallas as pl
from jax.experimental.pallas import tpu as pltpu

HEAD_DIM = 64
MOBA_BLOCK = 256
MOBA_TOPK = 3
MOBA_UNROLL = 6
PROJ_ROWS = 256
N_KV_B = 2
WINDOW = 128
N_EXPERTS = 8
MOE_TOPK = 2
DEPTH = 2
DEEPNORM_ALPHA = (2 * DEPTH) ** 0.25
LN_EPS = 1e-5
NEG_INF = -1e30
PICKED = -3e38
LOG2E = 1.4426950408889634
VMEM_LIMIT = 56 * 1024 * 1024

F32 = jnp.float32
BF16 = jnp.bfloat16


def _cparams(*sem):
    return pltpu.CompilerParams(dimension_semantics=sem, vmem_limit_bytes=VMEM_LIMIT)


def _layer_norm(y, g, b):
    mu = jnp.mean(y, axis=-1, keepdims=True)
    yc = y - mu
    var = jnp.mean(yc * yc, axis=-1, keepdims=True)
    return yc * lax.rsqrt(var + LN_EPS) * g + b


_NT = (((1,), (1,)), ((), ()))
_TN = (((0,), (0,)), ((), ()))


def _qkv_kernel(x_ref, wqT_ref, wk_ref, wvT_ref, qT_ref, k_ref, vT_ref, *, q_groups):
    xb = x_ref[...].astype(BF16)
    k_ref[...] = jnp.dot(xb, wk_ref[...], preferred_element_type=F32).astype(BF16)
    qT = lax.dot_general(wqT_ref[...], xb, _NT, preferred_element_type=F32).astype(BF16)
    vT = lax.dot_general(wvT_ref[...], xb, _NT, preferred_element_type=F32).astype(BF16)
    _, hv, nblk, hd, bs = vT_ref.shape
    _, hq, _, padded, ql = qT_ref.shape
    pad = padded // hd
    zero = jnp.zeros((hd, bs), BF16)
    for blk in range(nblk):
        cols = slice(blk * bs, (blk + 1) * bs)
        for h in range(hv):
            vT_ref[0, h, blk] = vT[h * hd:(h + 1) * hd, cols]
        for h in range(hq):
            lanes = []
            for g in range(q_groups):
                r0 = (h * q_groups + g) * hd
                parts = [zero] * pad
                parts[h % pad] = qT[r0:r0 + hd, cols]
                lanes.append(jnp.concatenate(parts, axis=0))
            qT_ref[0, h, blk] = lanes[0] if q_groups == 1 else jnp.concatenate(lanes, axis=1)


def qkv_proj(x, wqT, wk, wvT, batch, hq, hv, hd, bs, pad, q_groups, tm=1024):
    n, d = x.shape
    t = n // batch
    tm = min(tm, t)
    nblk = tm // bs
    steps = t // tm
    whole = lambda a: pl.BlockSpec(a.shape, lambda b, i: (0, 0))
    return pl.pallas_call(
        functools.partial(_qkv_kernel, q_groups=q_groups),
        out_shape=(jax.ShapeDtypeStruct((batch, hq, t // bs, pad * hd, q_groups * bs), BF16),
                   jax.ShapeDtypeStruct((n, wk.shape[1]), BF16),
                   jax.ShapeDtypeStruct((batch, hv, t // bs, hd, bs), BF16)),
        grid=(batch, steps),
        in_specs=[pl.BlockSpec((tm, d), lambda b, i: (b * steps + i, 0)), whole(wqT), whole(wk), whole(wvT)],
        out_specs=(pl.BlockSpec((1, hq, nblk, pad * hd, q_groups * bs), lambda b, i: (b, 0, i, 0, 0)),
                   pl.BlockSpec((tm, wk.shape[1]), lambda b, i: (b * steps + i, 0)),
                   pl.BlockSpec((1, hv, nblk, hd, bs), lambda b, i: (b, 0, i, 0, 0))),
        compiler_params=_cparams("parallel", "parallel"),
        name="qkv_proj",
    )(x, wqT, wk, wvT)


def _oT_proj_ln_kernel(oT_ref, w_ref, x_ref, g_ref, b_ref, *rest, bs, then):
    if then == "ffn":
        wg_ref, wu_ref, wd_ref, g2_ref, b2_ref, o_ref = rest
    elif then == "route":
        wrT_ref, o_ref, gate_ref, rank_ref, count_ref = rest
    else:
        o_ref, = rest
    _, ho, nblk, hd, lanes = oT_ref.shape
    per = max(1, PROJ_ROWS // bs)
    mixes = []
    for first in range(0, nblk, per):
        tiles = []
        for blk in range(first, first + per):
            heads = [oT_ref[0, h, blk, :, g * bs:(g + 1) * bs] for h in range(ho) for g in range(lanes // bs)]
            tiles.append(jnp.concatenate(heads, axis=0))
        aT = tiles[0] if per == 1 else jnp.concatenate(tiles, axis=1)
        mixes.append(lax.dot_general(aT, w_ref[...], _TN, preferred_element_type=F32))
    for i, mix in enumerate(mixes):
        rows = pl.ds(i * per * bs, per * bs)
        y = DEEPNORM_ALPHA * x_ref[rows, :] + mix
        o_ref[rows, :] = _layer_norm(y, g_ref[...], b_ref[...])
    if then == "ffn":
        x1 = o_ref[...]
        ffn = _swiglu(x1.astype(BF16), wg_ref[...], wu_ref[...], wd_ref[...])
        o_ref[...] = _layer_norm(DEEPNORM_ALPHA * x1 + ffn, g2_ref[...], b2_ref[...])
    elif then == "route":
        gate_ref[...], rank_ref[...], count = _route(o_ref[...], wrT_ref[...])
        count_ref[0] = jnp.broadcast_to(count, count_ref.shape[1:])


def oT_proj_ln(oT, w, x, g, b, bs, tm, ffn=None, router_wT=None):
    batch, ho, nb, hd, lanes = oT.shape
    n, d = x.shape
    t = n // batch
    tm = min(tm, t)
    nblk = tm // bs
    steps = t // tm
    tile = pl.BlockSpec((tm, d), lambda bi, i: (bi * steps + i, 0))
    const = lambda a: pl.BlockSpec(a.shape, lambda bi, i: (0,) * a.ndim, pipeline_mode=pl.Buffered(1))
    args = [oT, w, x, g.reshape(1, d), b.reshape(1, d)]
    in_specs = [pl.BlockSpec((1, ho, nblk, hd, lanes), lambda bi, i: (bi, 0, i, 0, 0)), const(w), tile,
                const(args[3]), const(args[4])]
    out_shape, out_specs, then = jax.ShapeDtypeStruct((n, d), F32), tile, None
    if ffn is not None:
        wg, wu, wd, g2, b2 = ffn
        extra = [wg, wu, wd, g2.reshape(1, d), b2.reshape(1, d)]
        then = "ffn"
    elif router_wT is not None:
        ne = router_wT.shape[0]
        extra = [router_wT]
        then = "route"
        out_shape = (out_shape, jax.ShapeDtypeStruct((ne, n), F32), jax.ShapeDtypeStruct((ne, n), F32),
                     jax.ShapeDtypeStruct((n // tm, ne, 128), F32))
        lane_tile = pl.BlockSpec((ne, tm), lambda bi, i: (0, bi * steps + i))
        out_specs = (tile, lane_tile, lane_tile, pl.BlockSpec((1, ne, 128), lambda bi, i: (bi * steps + i, 0, 0)))
    else:
        extra = []
    return pl.pallas_call(
        functools.partial(_oT_proj_ln_kernel, bs=bs, then=then),
        out_shape=out_shape,
        grid=(batch, steps),
        in_specs=in_specs + [const(a) for a in extra],
        out_specs=out_specs,
        compiler_params=_cparams("parallel", "parallel"),
        name="proj_ln" + ("_" + then if then else ""),
    )(*args, *extra)


def _moba_kernel(slope_ref, qblk_ref, kblk_ref, qT_ref, k_ref, vT_ref, o_ref,
                 sel_ref, bias_ref, z_ref, p_ref, acc_ref, st_ref):
    _, hg, nb, hd, bs = vT_ref.shape
    pair = qT_ref.shape[3]
    n_steps = nb * (nb + 1) // 2
    heads = range(hg)
    slopes = [slope_ref[pl.program_id(1) * hg + g] * LOG2E for g in heads]

    key_i = lax.broadcasted_iota(jnp.int32, (bs, bs), 0)
    qry_i = lax.broadcasted_iota(jnp.int32, (bs, bs), 1)
    rel = (qry_i - key_i).astype(F32)
    blk_i = lax.broadcasted_iota(jnp.int32, (nb, bs), 0)
    klanes = [pl.ds((g * hd) // pair * pair, pair) for g in heads]
    kmean = jnp.sum(k_ref[...].astype(F32).reshape(nb, bs, hg * hd), axis=1) * (1.0 / bs)
    km_hi = kmean.astype(BF16)
    km_lo = (kmean - km_hi.astype(F32)).astype(BF16)
    km2 = []
    for g in heads:
        bias = slopes[g] * rel
        bias_ref[g, 0] = bias
        bias_ref[g, 1] = jnp.where(rel >= 0, bias, -NEG_INF)
        lo = (g * hd) // pair * pair
        km2.append(jnp.concatenate([km_hi[:, lo:lo + pair], km_lo[:, lo:lo + pair]], axis=0))

    def select(i, carry):
        past = blk_i < i
        own = jnp.where(blk_i == i, 1.0, 0.0)
        for g in heads:
            g2 = jnp.dot(km2[g], qT_ref[0, g, i], preferred_element_type=F32)
            gate = jnp.where(past, g2[:nb] + g2[nb:], NEG_INF)
            sel = jnp.zeros((nb, bs), F32)
            for _ in range(MOBA_TOPK):
                best = jnp.max(gate, axis=0, keepdims=True)
                first = jnp.min(jnp.where(gate == best, blk_i, nb), axis=0, keepdims=True)
                pick = blk_i == first
                sel = jnp.where(pick, 1.0, sel)
                gate = jnp.where(pick, PICKED, gate)
            sel_ref[g, i] = jnp.where(past, sel, own)
        return carry

    lax.fori_loop(0, nb, select, 0, unroll=4)

    ROW_M, ROW_A, ROW_ZMAX = 0, 1, 3
    ones_rows = jnp.ones((acc_ref.shape[1] - hd, bs), BF16)
    z_ref[...] = jnp.zeros_like(z_ref)
    p_ref[...] = jnp.zeros_like(p_ref)
    acc0 = jnp.where(lax.broadcasted_iota(jnp.int32, acc_ref.shape[1:], 0) < hd, 0.0, 1.0)
    st0 = jnp.where(lax.broadcasted_iota(jnp.int32, (8, bs), 0) == ROW_M, NEG_INF, 1.0)
    for g in heads:
        acc_ref[g] = acc0
        st_ref[g] = st0

    def step(s, par):
        sa = jnp.minimum(s, n_steps - 1)
        sb = jnp.clip(s - 1, 0, n_steps - 1)
        sc = jnp.clip(s - 2, 0, n_steps - 1)
        qa, ka = qblk_ref[sa], kblk_ref[sa]
        qb, kb = qblk_ref[sb], kblk_ref[sb]
        qc, kc = qblk_ref[sc], kblk_ref[sc]
        own_a = (qa == ka).astype(jnp.int32)
        own = qb == kb
        for g in heads:
            k_blk = k_ref[pl.ds(pl.multiple_of(ka * bs, bs), bs), klanes[g]]
            scores = jnp.dot(k_blk, qT_ref[0, g, qa], preferred_element_type=F32)
            z = scores - bias_ref[g, own_a]
            z_ref[par, g] = z
            st_ref[g, pl.ds(ROW_ZMAX + par, 1), :] = jnp.max(z, axis=0, keepdims=True)

            off = slopes[g] * lax.convert_element_type((qb - kb) * bs, F32)
            chosen = jnp.logical_and(sel_ref[g, qb, pl.ds(kb, 1), :] > 0.0, s <= n_steps)
            m_in = jnp.where(own, NEG_INF, st_ref[g, pl.ds(ROW_M, 1), :])
            tile_max = st_ref[g, pl.ds(ROW_ZMAX + 1 - par, 1), :]
            m_new = jnp.where(chosen, jnp.maximum(m_in, tile_max - off), m_in)
            p_f = jnp.exp2(z_ref[1 - par, g] - jnp.where(chosen, m_new + off, -NEG_INF))
            p_ref[1 - par, g] = p_f.astype(BF16)
            st_ref[g, pl.ds(ROW_M, 1), :] = m_new
            st_ref[g, pl.ds(ROW_A + 1 - par, 1), :] = jnp.exp2(m_in - m_new)

            a = st_ref[g, pl.ds(ROW_A + par, 1), :]
            v_aug = jnp.concatenate([vT_ref[0, g, kc], ones_rows], axis=0)
            acc_new = a * acc_ref[g] + jnp.dot(v_aug, p_ref[par, g], preferred_element_type=F32)
            acc_ref[g] = acc_new
            o_ref[0, g, qc] = (acc_new[:hd] / acc_new[hd:hd + 1]).astype(o_ref.dtype)

    def unrolled(t, carry):
        for u in range(MOBA_UNROLL):
            step(MOBA_UNROLL * t + u, u % 2)
        return carry

    lax.fori_loop(0, -(-(n_steps + 2) // MOBA_UNROLL), unrolled, 0)


def moba_attention(qT, k, vT, slopes, heads_per_step=4):
    b, h, nb, hd, bs = vT.shape
    hg = heads_per_step
    steps = [(i, i if t == 0 else t - 1) for i in range(nb) for t in range(i + 1)]
    qblk = jnp.asarray([s[0] for s in steps], jnp.int32)
    kblk = jnp.asarray([s[1] for s in steps], jnp.int32)
    blk = lambda s: pl.BlockSpec((1, hg) + s, lambda bi, hi, *_: (bi, hi, 0, 0, 0))
    return pl.pallas_call(
        _moba_kernel,
        out_shape=jax.ShapeDtypeStruct(vT.shape, BF16),
        grid_spec=pltpu.PrefetchScalarGridSpec(
            num_scalar_prefetch=3,
            grid=(b, h // hg),
            in_specs=[blk((nb, qT.shape[3], bs)),
                      pl.BlockSpec((nb * bs, hg * hd), lambda bi, hi, *_: (bi, hi)),
                      blk((nb, hd, bs))],
            out_specs=blk((nb, hd, bs)),
            scratch_shapes=[pltpu.VMEM((hg, nb, nb, bs), F32),
                            pltpu.VMEM((hg, 2, bs, bs), F32),
                            pltpu.VMEM((2, hg, bs, bs), F32),
                            pltpu.VMEM((2, hg, bs, bs), BF16),
                            pltpu.VMEM((hg, hd + 16, bs), F32),
                            pltpu.VMEM((hg, 8, bs), F32)]),
        compiler_params=_cparams("parallel", "parallel"),
        name="moba_attention",
    )(slopes, qblk, kblk, qT, k, vT)


def _swa_kernel(slope_ref, sink_ref, qT_ref, k_ref, vT_ref, o_ref, pbias_ref, cbias_ref, z_ref):
    nb = qT_ref.shape[2]
    w = vT_ref.shape[4]
    gw = qT_ref.shape[4]
    slope = slope_ref[0]
    sink = sink_ref[0]
    key_i = lax.broadcasted_iota(jnp.int32, (w, gw), 0)
    qry_i = lax.broadcasted_iota(jnp.int32, (w, gw), 1) % w
    d_prev = (w + qry_i - key_i).astype(F32)
    d_cur = (qry_i - key_i).astype(F32)
    pbias_ref[...] = jnp.where(d_prev < w, slope * d_prev, -NEG_INF)
    cbias_ref[...] = jnp.where(d_cur >= 0, slope * d_cur, -NEG_INF)

    def scores(n, slot):
        qT = qT_ref[0, 0, n]
        prev = jnp.maximum(n - 1, 0)
        no_prev = jnp.where(n == 0, -NEG_INF, 0.0)
        k_prev = k_ref[pl.ds(pl.multiple_of(prev * w, w), w), :]
        k_cur = k_ref[pl.ds(pl.multiple_of(n * w, w), w), :]
        z_ref[slot, 0] = jnp.dot(k_prev, qT, preferred_element_type=F32) - pbias_ref[...] - no_prev
        z_ref[slot, 1] = jnp.dot(k_cur, qT, preferred_element_type=F32) - cbias_ref[...]

    hd = vT_ref.shape[3]
    ones_rows = jnp.ones((16, w), BF16)

    def attend(n, slot):
        zp, zc = z_ref[slot, 0], z_ref[slot, 1]
        prev = jnp.maximum(n - 1, 0)
        m = jnp.maximum(jnp.max(zp, axis=0, keepdims=True), jnp.max(zc, axis=0, keepdims=True))
        m = jnp.maximum(m, sink)
        pp = jnp.exp2(zp - m).astype(BF16)
        pc = jnp.exp2(zc - m).astype(BF16)
        v_prev = jnp.concatenate([vT_ref[0, 0, prev], ones_rows], axis=0)
        v_cur = jnp.concatenate([vT_ref[0, 0, n], ones_rows], axis=0)
        acc = jnp.dot(v_prev, pp, preferred_element_type=F32) + jnp.dot(v_cur, pc, preferred_element_type=F32)
        l = acc[hd:hd + 1] + jnp.exp2(sink - m)
        o_ref[0, 0, n] = (acc[:hd] / l).astype(o_ref.dtype)

    scores(0, 0)

    def two_blocks(t, carry):
        n = 2 * t
        scores(n + 1, 1)
        attend(n, 0)
        scores(jnp.minimum(n + 2, nb - 1), 0)
        attend(n + 1, 1)
        return carry

    lax.fori_loop(0, nb // 2, two_blocks, 0)


def swa_attention(qT, k, vT, slope_lanes, sink_lanes):
    b, kv, nb, hd, w = vT.shape
    gw = qT.shape[4]
    idx = lambda bi, ki: (bi, ki, 0, 0, 0)
    lane = pl.BlockSpec((1, 1, gw), lambda bi, ki: (ki, 0, 0))
    return pl.pallas_call(
        _swa_kernel,
        out_shape=jax.ShapeDtypeStruct((b, kv, nb, hd, gw), BF16),
        grid=(b, kv),
        in_specs=[lane, lane,
                  pl.BlockSpec((1, 1, nb, qT.shape[3], gw), idx),
                  pl.BlockSpec((nb * w, k.shape[1]), lambda bi, ki: (bi, 0)),
                  pl.BlockSpec((1, 1, nb, hd, w), idx)],
        out_specs=pl.BlockSpec((1, 1, nb, hd, gw), idx),
        scratch_shapes=[pltpu.VMEM((w, gw), F32), pltpu.VMEM((w, gw), F32), pltpu.VMEM((2, 2, w, gw), F32)],
        compiler_params=_cparams("parallel", "parallel"),
        name="swa_attention",
    )(slope_lanes, sink_lanes, qT, k, vT)


def _swiglu(xb, wg, wu, wd):
    hg = jnp.dot(xb, wg, preferred_element_type=F32)
    hu = jnp.dot(xb, wu, preferred_element_type=F32)
    hh = hg * jax.nn.sigmoid(hg) * hu
    return jnp.dot(hh.astype(BF16), wd, preferred_element_type=F32)


def _route(x, wrT):
    tt = x.shape[0]
    logits = lax.dot_general(wrT, x, _NT, preferred_element_type=F32, precision=lax.Precision.HIGHEST)
    e_i = lax.broadcasted_iota(jnp.int32, logits.shape, 0)
    work = logits
    vals, picks = [], []
    for _ in range(MOE_TOPK):
        best = jnp.max(work, axis=0, keepdims=True)
        first = jnp.min(jnp.where(work == best, e_i, N_EXPERTS), axis=0, keepdims=True)
        pick = e_i == first
        vals.append(best)
        picks.append(pick)
        work = jnp.where(pick, PICKED, work)
    ex = [jnp.exp(v - vals[0]) for v in vals]
    denom = functools.reduce(lambda a, c: a + c, ex)
    gates = jnp.zeros(logits.shape, F32)
    sel = jnp.zeros(logits.shape, F32)
    for pick, e in zip(picks, ex):
        gates = jnp.where(pick, e / denom, gates)
        sel = jnp.where(pick, 1.0, sel)
    before = (lax.broadcasted_iota(jnp.int32, (tt, tt), 0) < lax.broadcasted_iota(jnp.int32, (tt, tt), 1))
    rank = jnp.dot(sel.astype(BF16), jnp.where(before, 1.0, 0.0).astype(BF16), preferred_element_type=F32)
    return gates, jnp.where(sel > 0.0, rank, -1.0), jnp.sum(sel, axis=1, keepdims=True)


MOE_BIG = 512
MOE_SMALL = (256, 288, 320)
MOE_TOKEN_TILE = 1024
MOE_FF_SPLIT = 2


def _moe_kernel(count_ref, x_ref, gate_ref, rank_ref, wg_ref, wu_ref, wd_ref, g_ref, b_ref, o_ref,
                xb_ref, xe_ref, ye_ref):
    t, e, half = pl.program_id(0), pl.program_id(1), pl.program_id(2)
    ne, nh = pl.num_programs(1), pl.num_programs(2)
    tt = x_ref.shape[0]
    count = count_ref[t * ne + e]
    n_big = count // MOE_BIG
    rest = count - n_big * MOE_BIG
    rank_row = rank_ref[pl.ds(e, 1), :]

    def one_hot(r0, m):
        row_i = lax.broadcasted_iota(jnp.int32, (m, tt), 0).astype(F32)
        return rank_row == row_i + lax.convert_element_type(r0, F32)

    def row_groups(fn, sizes=MOE_SMALL[-1:]):
        def body(i, carry):
            fn(pl.multiple_of(i * MOE_BIG, MOE_BIG), MOE_BIG)
            return carry
        lax.fori_loop(0, n_big + (rest > MOE_SMALL[-1]).astype(jnp.int32), body, 0)
        lo = 0
        for m in sizes:
            @pl.when(jnp.logical_and(rest > lo, rest <= m))
            def _():
                fn(pl.multiple_of(n_big * MOE_BIG, MOE_BIG), m)
            lo = m

    @pl.when(jnp.logical_and(e == 0, half == 0))
    def _():
        o_ref[...] = jnp.zeros_like(o_ref)
        xb_ref[...] = x_ref[...].astype(BF16)

    @pl.when(half == 0)
    def _():
        def gather(r0, m):
            hot = jnp.where(one_hot(r0, m), 1.0, 0.0).astype(BF16)
            xe_ref[pl.ds(r0, m), :] = jnp.dot(hot, xb_ref[...], preferred_element_type=F32).astype(BF16)
            ye_ref[pl.ds(r0, m), :] = jnp.zeros((m, ye_ref.shape[1]), F32)
        row_groups(gather)

    def expert(r0, m):
        ye_ref[pl.ds(r0, m), :] += _swiglu(xe_ref[pl.ds(r0, m), :], wg_ref[0], wu_ref[0], wd_ref[0])
    row_groups(expert, MOE_SMALL)

    @pl.when(half == nh - 1)
    def _():
        gate_row = gate_ref[pl.ds(e, 1), :]

        def scatter(r0, m):
            hot = one_hot(r0, m)
            w_col = jnp.sum(jnp.where(hot, gate_row, 0.0), axis=1, keepdims=True)
            yw = (ye_ref[pl.ds(r0, m), :] * w_col).astype(BF16)
            o_ref[...] += lax.dot_general(jnp.where(hot, 1.0, 0.0).astype(BF16), yw,
                                          (((0,), (0,)), ((), ())), preferred_element_type=F32)
        row_groups(scatter, (MOE_SMALL[0], MOE_SMALL[-1]))

    @pl.when(jnp.logical_and(e == ne - 1, half == nh - 1))
    def _():
        y = DEEPNORM_ALPHA * x_ref[...] + o_ref[...]
        o_ref[...] = _layer_norm(y, g_ref[...], b_ref[...])


def moe_ffn_ln(x, gate, rank, count, wg, wu, wd, g, b, tt):
    n, d = x.shape
    ne, fh = wg.shape[0], wg.shape[2] // MOE_FF_SPLIT
    w_cols = pl.BlockSpec((1, d, fh), lambda t, e, h, *_: (e, 0, h))
    w_rows = pl.BlockSpec((1, fh, d), lambda t, e, h, *_: (e, h, 0))
    row = pl.BlockSpec((1, d), lambda t, e, h, *_: (0, 0))
    return pl.pallas_call(
        _moe_kernel,
        out_shape=jax.ShapeDtypeStruct((n, d), F32),
        grid_spec=pltpu.PrefetchScalarGridSpec(
            num_scalar_prefetch=1,
            grid=(n // tt, ne, MOE_FF_SPLIT),
            in_specs=[pl.BlockSpec((tt, d), lambda t, e, h, *_: (t, 0)),
                      pl.BlockSpec((ne, tt), lambda t, e, h, *_: (0, t)),
                      pl.BlockSpec((ne, tt), lambda t, e, h, *_: (0, t)),
                      w_cols, w_cols, w_rows, row, row],
            out_specs=pl.BlockSpec((tt, d), lambda t, e, h, *_: (t, 0)),
            scratch_shapes=[pltpu.VMEM((tt, d), BF16), pltpu.VMEM((tt, d), BF16), pltpu.VMEM((tt, d), F32)]),
        compiler_params=_cparams("parallel", "arbitrary", "arbitrary"),
        name="moe_ffn_ln",
    )(count, x, gate, rank, wg, wu, wd, g.reshape(1, d), b.reshape(1, d))


def _alibi_slopes(n):
    return jnp.exp2(-8.0 * jnp.arange(1, n + 1, dtype=F32) / n)


def kernel(x, w_qkv_a, w_o_a, w_kv_shared, w_q_b, w_o_b, sinks_b, w_gate_d, w_up_d, w_down_d,
           w_router, w_gate_e, w_up_e, w_down_e, ln_gain, ln_bias):
    B, T, D = x.shape
    hd = HEAD_DIM
    H = D // hd
    N = B * T
    xs = x.reshape(N, D)
    q_scale = hd ** -0.5

    HA = w_qkv_a.shape[-1] // 3
    wq, wk, wv = (w_qkv_a[0][:, i * HA:(i + 1) * HA] for i in range(3))
    qT, k, vT = qkv_proj(xs, (wq * (q_scale * LOG2E)).T.astype(BF16), wk.astype(BF16), wv.T.astype(BF16),
                         batch=B, hq=H, hv=H, hd=hd, bs=MOBA_BLOCK, pad=2, q_groups=1)
    oT = moba_attention(qT, k, vT, _alibi_slopes(H))
    dense = (w_gate_d[0].astype(BF16), w_up_d[0].astype(BF16), w_down_d[0].astype(BF16), ln_gain[0, 1], ln_bias[0, 1])
    xs = oT_proj_ln(oT, w_o_a[0].astype(BF16), xs, ln_gain[0, 0], ln_bias[0, 0], MOBA_BLOCK, 512, ffn=dense)

    KV = N_KV_B
    G = H // KV
    W = WINDOW
    wk, wv = w_kv_shared[:, :KV * hd], w_kv_shared[:, KV * hd:]
    qT, k, vT = qkv_proj(xs, (w_q_b[0] * (q_scale * LOG2E)).T.astype(BF16), wk.astype(BF16), wv.T.astype(BF16),
                         batch=B, hq=KV, hv=KV, hd=hd, bs=W, pad=KV, q_groups=G)
    slope_lanes = jnp.repeat(_alibi_slopes(H).reshape(KV, G) * LOG2E, W, axis=1).reshape(KV, 1, G * W)
    sink_lanes = jnp.repeat(sinks_b[0].astype(F32).reshape(KV, G) * LOG2E, W, axis=1).reshape(KV, 1, G * W)
    oT = swa_attention(qT, k, vT, slope_lanes, sink_lanes)
    tt = min(MOE_TOKEN_TILE, T)
    xs, gate, rank, count = oT_proj_ln(oT, w_o_b[0].astype(BF16), xs, ln_gain[1, 0], ln_bias[1, 0], W, tt,
                                       router_wT=w_router[0].T)

    count = count[:, :, 0].astype(jnp.int32).reshape(-1)
    xs = moe_ffn_ln(xs, gate, rank, count, w_gate_e[0].astype(BF16), w_up_e[0].astype(BF16),
                    w_down_e[0].astype(BF16), ln_gain[1, 1], ln_bias[1, 1], tt)
    return xs.reshape(B, T, D)
```
